```python
import jax
import jax.numpy as jnp
from jax import lax
import numpy as np

D_MODEL = 1024
BATCH = 16
SEQ = 4096
DEPTH = 2
DEC_BATCH = 1
DEC_SEQ = 16384
PAST_LEN = 128

N_MIXERS = 4
Q_HEADS = 4
KV_HEADS = 2
GQA_GROUP = Q_HEADS // KV_HEADS
HEAD_DIM = D_MODEL // (N_MIXERS * Q_HEADS)
MIX_WIDTH = N_MIXERS * Q_HEADS * HEAD_DIM
KV_WIDTH = N_MIXERS * KV_HEADS * HEAD_DIM
IN_WIDTH = MIX_WIDTH + 2 * KV_WIDTH
D_FF = 2816
CONV_W = 3
GRID_W = 64
WIN_A = 128
BLOCK_A = 128
NA_ROWS = 8
NA_COLS = 16
DILATIONS = ((128, 1), (512, 4), (2048, 16))
BLOCK_C = 64
BLOCK_D = 128
ROPE_THETA = 10000.0
ROT_ROW = HEAD_DIM // 2
ROT_COL = HEAD_DIM - ROT_ROW
EPS = 1e-6
NEG_INF = -1e30

kernel_name = 'hybrid_parallel_head_encoder'


def rms_norm(x, w):
    xf = x.astype(jnp.float32)
    y = xf * lax.rsqrt(jnp.mean(xf * xf, axis=-1, keepdims=True) + EPS)
    return (y * w.astype(jnp.float32)).astype(x.dtype)


def alibi_slopes():
    n = 2 * Q_HEADS
    s = 2.0 ** (-8.0 * jnp.arange(1, n + 1, dtype=jnp.float32) / n)
    return s[0::2].reshape(KV_HEADS, GQA_GROUP), s[1::2].reshape(KV_HEADS, GQA_GROUP)


def alibi_bias(slopes, step):
    def bias(dist):
        return -slopes[:, :, None, None] * (step * dist).astype(jnp.float32)
    return bias


def banded_attention(q, k, v, radius, block, bias_fn, sink=None):
    n, length, hkv, g, dh = q.shape
    nblk = -(-length // block)
    lp = nblk * block
    wk = block + 2 * radius
    qp = jnp.pad(q, ((0, 0), (0, lp - length), (0, 0), (0, 0), (0, 0)))
    kv_pad = ((0, 0), (radius, lp - length + radius), (0, 0), (0, 0))
    kp = jnp.pad(k, kv_pad)
    vp = jnp.pad(v, kv_pad)
    scale = dh ** -0.5

    def one_block(i):
        start = i * block
        qb = lax.dynamic_slice_in_dim(qp, start, block, axis=1)
        kb = lax.dynamic_slice_in_dim(kp, start, wk, axis=1)
        vb = lax.dynamic_slice_in_dim(vp, start, wk, axis=1)
        qpos = start + jnp.arange(block)
        kpos = start - radius + jnp.arange(wk)
        dist = jnp.abs(qpos[:, None] - kpos[None, :])
        valid = (dist <= radius) & (kpos >= 0)[None, :] & (kpos < length)[None, :]
        s = jnp.einsum('nqhgd,nkhd->nhgqk', qb, kb, preferred_element_type=jnp.float32) * scale
        s = jnp.where(valid, s + bias_fn(dist), NEG_INF)
        m = jnp.max(s, axis=-1)
        if sink is not None:
            m = jnp.maximum(m, sink[:, :, None].astype(jnp.float32))
        p = jnp.exp(s - m[..., None])
        den = jnp.sum(p, axis=-1)
        if sink is not None:
            den = den + jnp.exp(sink[:, :, None].astype(jnp.float32) - m)
        o = jnp.einsum('nhgqk,nkhd->nqhgd', p.astype(vb.dtype), vb, preferred_element_type=jnp.float32)
        o = o / jnp.transpose(den, (0, 3, 1, 2))[..., None]
        lse = jnp.transpose(m + jnp.log(den), (0, 3, 1, 2))
        return o.astype(q.dtype), lse

    o, lse = lax.map(one_block, jnp.arange(nblk))
    o = jnp.moveaxis(o, 0, 1).reshape(n, lp, hkv, g, dh)[:, :length]
    lse = jnp.moveaxis(lse, 0, 1).reshape(n, lp, hkv, g)[:, :length]
    return o, lse


def neighbourhood_attention(q, k, v, rpb):
    b, s_len, hkv, g, dh = q.shape
    rows = s_len // GRID_W
    kr = min(NA_ROWS, rows)
    kc = NA_COLS
    qg = q.reshape(b, rows, GRID_W, hkv, g, dh)
    kg = k.reshape(b, rows, GRID_W, hkv, dh)
    vg = v.reshape(b, rows, GRID_W, hkv, dh)
    col = jnp.arange(GRID_W)
    col_idx = jnp.clip(col - kc // 2, 0, GRID_W - kc)[:, None] + jnp.arange(kc)[None, :]
    dc = col_idx - col[:, None] + (NA_COLS - 1)
    scale = dh ** -0.5

    def one_row(r):
        r0 = jnp.clip(r - kr // 2, 0, rows - kr)
        k_rows = lax.dynamic_slice_in_dim(kg, r0, kr, axis=1)
        v_rows = lax.dynamic_slice_in_dim(vg, r0, kr, axis=1)
        kn = k_rows[:, :, col_idx]
        vn = v_rows[:, :, col_idx]
        qr = lax.dynamic_index_in_dim(qg, r, axis=1, keepdims=False)
        s = jnp.einsum('bqhgd,brqchd->bhgqrc', qr, kn, preferred_element_type=jnp.float32) * scale
        dr = r0 + jnp.arange(kr) - r + (NA_ROWS - 1)
        s = s + rpb[:, :, dr[None, :, None], dc[:, None, :]].astype(jnp.float32)
        p = jax.nn.softmax(s.reshape(b, hkv, g, GRID_W, kr * kc), axis=-1).reshape(s.shape)
        o = jnp.einsum('bhgqrc,brqchd->bqhgd', p.astype(vn.dtype), vn, preferred_element_type=jnp.float32)
        return o.astype(q.dtype)

    o = lax.map(one_row, jnp.arange(rows))
    return jnp.moveaxis(o, 0, 1).reshape(b, s_len, hkv, g, dh)


def dilated_attention(q, k, v, slopes):
    b, s_len = q.shape[:2]
    outs, lses = [], []
    for window, dil in DILATIONS:
        sub = s_len // dil

        def to_strided(t):
            t = jnp.moveaxis(t.reshape((b, sub, dil) + t.shape[2:]), 2, 1)
            return t.reshape((b * dil, sub) + t.shape[3:])

        def from_strided(t):
            t = jnp.moveaxis(t.reshape((b, dil, sub) + t.shape[2:]), 1, 2)
            return t.reshape((b, s_len) + t.shape[3:])

        o, lse = banded_attention(to_strided(q), to_strided(k), to_strided(v),
                                  window // (2 * dil), BLOCK_C, alibi_bias(slopes, dil))
        outs.append(from_strided(o))
        lses.append(from_strided(lse))
    wts = jax.nn.softmax(jnp.stack(lses), axis=0)
    out = jnp.einsum('rbshg,rbshgd->bshgd', wts, jnp.stack(outs).astype(jnp.float32))
    return out.astype(q.dtype)


def axial_rope(x):
    s_len, dh = x.shape[1], x.shape[-1]
    t = jnp.arange(s_len)
    row = (t // GRID_W).astype(jnp.float32)
    col = (t % GRID_W).astype(jnp.float32)
    f_row = ROPE_THETA ** (-jnp.arange(0, ROT_ROW, 2, dtype=jnp.float32) / ROT_ROW)
    f_col = ROPE_THETA ** (-jnp.arange(0, ROT_COL, 2, dtype=jnp.float32) / ROT_COL)
    ang = jnp.concatenate([row[:, None] * f_row[None, :], col[:, None] * f_col[None, :]], axis=-1)
    shape = (s_len,) + (1,) * (x.ndim - 3) + (dh // 2,)
    cos = jnp.cos(ang).reshape(shape)
    sin = jnp.sin(ang).reshape(shape)
    xf = x.astype(jnp.float32).reshape(x.shape[:-1] + (dh // 2, 2))
    xe, xo = xf[..., 0], xf[..., 1]
    out = jnp.stack([xe * cos - xo * sin, xe * sin + xo * cos], axis=-1)
    return out.reshape(x.shape).astype(x.dtype)


def dense_attention(q, k, v):
    b, s_len, hkv, g, dh = q.shape
    nb = s_len // BLOCK_D
    qb = jnp.moveaxis(q.reshape(b, nb, BLOCK_D, hkv, g, dh), 1, 0)
    scale = dh ** -0.5

    def one_block(qi):
        s = jnp.einsum('bqhgd,bkhd->bhgqk', qi, k, preferred_element_type=jnp.float32) * scale
        p = jax.nn.softmax(s, axis=-1)
        o = jnp.einsum('bhgqk,bkhd->bqhgd', p.astype(v.dtype), v, preferred_element_type=jnp.float32)
        return o.astype(q.dtype)

    o = lax.map(one_block, qb)
    return jnp.moveaxis(o, 0, 1).reshape(q.shape)


def mixer_layer(x, norm_w, w_in, q_norm_w, k_norm_w, sink_a, rpb_b, out_norm_w, w_out):
    b, s_len, _ = x.shape
    h = rms_norm(x, norm_w)
    proj = jnp.einsum('bsd,de->bse', h, w_in)
    q = proj[..., :MIX_WIDTH].reshape(b, s_len, N_MIXERS, KV_HEADS, GQA_GROUP, HEAD_DIM)
    k = proj[..., MIX_WIDTH:MIX_WIDTH + KV_WIDTH].reshape(b, s_len, N_MIXERS, KV_HEADS, HEAD_DIM)
    v = proj[..., MIX_WIDTH + KV_WIDTH:].reshape(b, s_len, N_MIXERS, KV_HEADS, HEAD_DIM)
    q = rms_norm(q, q_norm_w[:, None, None, :])
    k = rms_norm(k, k_norm_w[:, None, :])
    slopes_a, slopes_c = alibi_slopes()
    o_a = banded_attention(q[:, :, 0], k[:, :, 0], v[:, :, 0], WIN_A, BLOCK_A,
                           alibi_bias(slopes_a, 1), sink_a.reshape(KV_HEADS, GQA_GROUP))[0]
    o_b = neighbourhood_attention(q[:, :, 1], k[:, :, 1], v[:, :, 1],
                                  rpb_b.reshape(KV_HEADS, GQA_GROUP, 2 * NA_ROWS - 1, 2 * NA_COLS - 1))
    o_c = dilated_attention(q[:, :, 2], k[:, :, 2], v[:, :, 2], slopes_c)
    o_d = dense_attention(axial_rope(q[:, :, 3]), axial_rope(k[:, :, 3]), v[:, :, 3])
    o = jnp.stack([o_a, o_b, o_c, o_d], axis=2).reshape(b, s_len, N_MIXERS, Q_HEADS * HEAD_DIM)
    o = rms_norm(o, out_norm_w.reshape(N_MIXERS, Q_HEADS * HEAD_DIM)).reshape(b, s_len, MIX_WIDTH)
    return x + jnp.einsum('bse,ed->bsd', o, w_out)


def conv_glu_ffn(x, norm_w, w_gate, w_val, conv_w, conv_b, w_down):
    h = rms_norm(x, norm_w)
    gate = jnp.einsum('bsd,df->bsf', h, w_gate)
    val = jnp.einsum('bsd,df->bsf', h, w_val)
    gate = lax.conv_general_dilated(gate, conv_w[:, None, :].astype(gate.dtype), window_strides=(1,),
                                    padding=((CONV_W // 2, CONV_W // 2),),
                                    dimension_numbers=('NWC', 'WIO', 'NWC'),
                                    feature_group_count=D_FF) + conv_b
    y = jax.nn.gelu(gate, approximate=False) * val
    return x + jnp.einsum('bsf,fd->bsd', y, w_down)


def trunk(x, norm1_w, w_in, q_norm_w, k_norm_w, sink_a, rpb_b, out_norm_w, w_out,
          norm2_w, w_gate, w_val, conv_w, conv_b, w_down):
    for l in range(DEPTH):
        x = mixer_layer(x, norm1_w[l], w_in[l], q_norm_w[l], k_norm_w[l], sink_a[l], rpb_b[l],
                        out_norm_w[l], w_out[l])
        x = conv_glu_ffn(x, norm2_w[l], w_gate[l], w_val[l], conv_w[l], conv_b[l], w_down[l])
    return x


def setup_inputs(seed: int = 0) -> dict:
    key = jax.random.key(seed)
    ks = jax.random.split(key, 16)
    nrm = jax.random.normal
    d, f = D_MODEL, D_FF
    return {
        'x_prompt': nrm(ks[0], (BATCH, SEQ, d), jnp.float32),
        'x_sample': nrm(ks[1], (DEC_BATCH, DEC_SEQ, d), jnp.float32),
        'norm1_w': 1.0 + 0.02 * nrm(ks[2], (DEPTH, d), jnp.float32),
        'w_in': nrm(ks[3], (DEPTH, d, IN_WIDTH), jnp.float32) * d ** -0.5,
        'q_norm_w': 1.0 + 0.02 * nrm(ks[4], (DEPTH, N_MIXERS, HEAD_DIM), jnp.float32),
        'k_norm_w': 1.0 + 0.02 * nrm(ks[5], (DEPTH, N_MIXERS, HEAD_DIM), jnp.float32),
        'sink_a': 0.5 * nrm(ks[6], (DEPTH, Q_HEADS), jnp.float32),
        'rpb_b': 0.1 * nrm(ks[7], (DEPTH, Q_HEADS, 2 * NA_ROWS - 1, 2 * NA_COLS - 1), jnp.float32),
        'out_norm_w': 1.0 + 0.02 * nrm(ks[8], (DEPTH, MIX_WIDTH), jnp.float32),
        'w_out': nrm(ks[9], (DEPTH, MIX_WIDTH, d), jnp.float32) * MIX_WIDTH ** -0.5,
        'norm2_w': 1.0 + 0.02 * nrm(ks[10], (DEPTH, d), jnp.float32),
        'w_gate': nrm(ks[11], (DEPTH, d, f), jnp.float32) * d ** -0.5,
        'w_val': nrm(ks[12], (DEPTH, d, f), jnp.float32) * d ** -0.5,
        'conv_w': nrm(ks[13], (DEPTH, CONV_W, f), jnp.float32) * CONV_W ** -0.5,
        'conv_b': 0.01 * nrm(ks[14], (DEPTH, f), jnp.float32),
        'w_down': nrm(ks[15], (DEPTH, f, d), jnp.float32) * f ** -0.5,
    }


def reference(x_prompt, x_sample, norm1_w, w_in, q_norm_w, k_norm_w, sink_a, rpb_b, out_norm_w,
              w_out, norm2_w, w_gate, w_val, conv_w, conv_b, w_down):
    y_prompt = trunk(x_prompt, norm1_w, w_in, q_norm_w, k_norm_w, sink_a, rpb_b, out_norm_w, w_out,
                     norm2_w, w_gate, w_val, conv_w, conv_b, w_down)
    y_sample = trunk(x_sample, norm1_w, w_in, q_norm_w, k_norm_w, sink_a, rpb_b, out_norm_w, w_out,
                     norm2_w, w_gate, w_val, conv_w, conv_b, w_down)
    return (y_prompt, y_sample)
```

```python
import functools

import numpy as np
import jax
import jax.numpy as jnp
from jax import lax
from jax.experimental import pallas as pl
from jax.experimental.pallas import tpu as pltpu

D_MODEL = 1024
N_MIXERS = 4
Q_HEADS = 4
KV_HEADS = 2
HEAD_DIM = 64
MIX_WIDTH = N_MIXERS * Q_HEADS * HEAD_DIM
KV_WIDTH = N_MIXERS * KV_HEADS * HEAD_DIM
IN_WIDTH = MIX_WIDTH + 2 * KV_WIDTH
MIXER_Q = Q_HEADS * HEAD_DIM
MIXER_KV = KV_HEADS * HEAD_DIM
D_FF = 2816
GRID_W = 64
WIN_A = 128
NA_ROWS = 8
NA_COLS = 16
DILATIONS = ((128, 1), (512, 4), (2048, 16))
ROPE_THETA = 10000.0
EPS = 1e-6
NEG_INF = -1e30

LANES = 128
Q_SLOTS = Q_HEADS * LANES
FF_CHUNK = 256
N_FF_CHUNKS = D_FF // FF_CHUNK
HALO = 8
VMEM_LIMIT = 56 * 1024 * 1024

F32 = jnp.float32
BF16 = jnp.bfloat16


def _cparams(sem):
    return pltpu.CompilerParams(dimension_semantics=sem, vmem_limit_bytes=VMEM_LIMIT)


def _resident(shape):
    nd = len(shape)
    return pl.BlockSpec(shape, lambda *_: (0,) * nd, pipeline_mode=pl.Buffered(1))


ROPE_CHUNKS = (6, 7, 11)


def _qkv_kernel(x_ref, nw_ref, w_ref, gain_ref, cos_ref, sin_ref, q_ref, k_ref, v_ref):
    x = x_ref[...]
    ms = jnp.mean(x * x, axis=-1, keepdims=True)
    h = (x * lax.rsqrt(ms + EPS) * nw_ref[...]).astype(BF16)
    y = jnp.dot(h, w_ref[...], preferred_element_type=F32)
    lane = lax.broadcasted_iota(jnp.int32, (1, LANES), 1)
    lo = lane < HEAD_DIM
    first_half = (lane % HEAD_DIM) < HEAD_DIM // 2
    cos = cos_ref[...]
    sin = sin_ref[...]
    n_q = MIX_WIDTH // LANES
    for j in range((MIX_WIDTH + KV_WIDTH) // LANES):
        c = y[:, j * LANES:(j + 1) * LANES]
        c2 = c * c
        tot = jnp.sum(c2, axis=-1, keepdims=True)
        s_lo = jnp.sum(jnp.where(lo, c2, 0.0), axis=-1, keepdims=True)
        msq = jnp.where(lo, s_lo, tot - s_lo) * (1.0 / HEAD_DIM)
        cn = c * lax.rsqrt(msq + EPS) * gain_ref[:, j * LANES:(j + 1) * LANES]
        if j in ROPE_CHUNKS:
            partner = jnp.where(first_half,
                                pltpu.roll(cn, LANES - HEAD_DIM // 2, 1),
                                pltpu.roll(cn, HEAD_DIM // 2, 1))
            cn = cn * cos + partner * sin
        if j < n_q:
            mixer, g = divmod(j, 2)
            base = mixer * Q_SLOTS
            q_ref[:, base + g * LANES:base + (g + 1) * LANES] = jnp.where(lo, cn, 0.0).astype(BF16)
            q_ref[:, base + (2 + g) * LANES:base + (3 + g) * LANES] = jnp.where(lo, 0.0, cn).astype(BF16)
        else:
            k_ref[:, (j - n_q) * LANES:(j - n_q + 1) * LANES] = cn.astype(BF16)
    v_ref[...] = y[:, MIX_WIDTH + KV_WIDTH:].astype(BF16)


def _qkv_proj(x2, nw, w, gain, cos, sin, seq, tm):
    n = x2.shape[0]
    tiles_per_seq = seq // tm
    return pl.pallas_call(
        _qkv_kernel,
        grid=(n // tm,),
        in_specs=[
            pl.BlockSpec((tm, D_MODEL), lambda i: (i, 0)),
            _resident((1, D_MODEL)),
            _resident((D_MODEL, IN_WIDTH)),
            _resident((1, MIX_WIDTH + KV_WIDTH)),
            pl.BlockSpec((tm, LANES), lambda i: (i % tiles_per_seq, 0)),
            pl.BlockSpec((tm, LANES), lambda i: (i % tiles_per_seq, 0)),
        ],
        out_specs=[
            pl.BlockSpec((tm, N_MIXERS * Q_SLOTS), lambda i: (i, 0)),
            pl.BlockSpec((tm, KV_WIDTH), lambda i: (i, 0)),
            pl.BlockSpec((tm, KV_WIDTH), lambda i: (i, 0)),
        ],
        out_shape=[
            jax.ShapeDtypeStruct((n, N_MIXERS * Q_SLOTS), BF16),
            jax.ShapeDtypeStruct((n, KV_WIDTH), BF16),
            jax.ShapeDtypeStruct((n, KV_WIDTH), BF16),
        ],
        compiler_params=_cparams(("parallel",)),
        name="qkv_proj",
    )(x2, nw, w, gain, cos, sin)


def _stack_heads(qs):
    return jnp.concatenate([qs[:, h * LANES:(h + 1) * LANES] for h in range(Q_HEADS)], axis=0)


def _unstack_heads(o, rows, lo):
    c0 = jnp.where(lo, o[0:rows], o[2 * rows:3 * rows])
    c1 = jnp.where(lo, o[rows:2 * rows], o[3 * rows:4 * rows])
    return c0, c1


def _local_kernel(*refs, qb, kw, tq, length, radius, nbr, has_sink, want_lse):
    it = iter(refs)
    q_ref, k_ref, v_ref, tab_ref = next(it), next(it), next(it), next(it)
    sink_ref = next(it) if has_sink else None
    o_ref = next(it)
    lse_ref = next(it) if want_lse else None
    i = pl.program_id(2)
    nsub = tq // qb
    nblk = length // qb
    lo = lax.broadcasted_iota(jnp.int32, (1, LANES), 1) < HEAD_DIM

    def body(sb, carry):
        blk = i * nsub + sb
        if nbr:
            rows = length // GRID_W
            r0 = jnp.clip(blk - NA_ROWS // 2, 0, rows - NA_ROWS)
            kstart = r0 * GRID_W
            var = blk - r0
        else:
            kstart = jnp.clip(blk * qb - radius, 0, length - kw)
            var = jnp.where(blk == 0, 0, jnp.where(blk == nblk - 1, 2, 1))
        kstart = pl.multiple_of(kstart, qb)
        qoff = pl.multiple_of(sb * qb, qb)
        qst = _stack_heads(q_ref[pl.ds(qoff, qb), :])
        kb = k_ref[pl.ds(kstart, kw), :]
        vb = v_ref[pl.ds(kstart, kw), :]
        s = lax.dot_general(qst, kb, (((1,), (1,)), ((), ())), preferred_element_type=F32)
        s = s + tab_ref[var]
        m = jnp.max(s, axis=-1, keepdims=True)
        if has_sink:
            sk = sink_ref[:, 0:1]
            m = jnp.maximum(m, sk)
        p = jnp.exp(s - m)
        den = jnp.sum(p, axis=-1, keepdims=True)
        if has_sink:
            den = den + jnp.exp(sk - m)
        acc = jnp.dot(p.astype(BF16), vb, preferred_element_type=F32)
        c0, c1 = _unstack_heads(acc / den, qb, lo)
        o_ref[pl.ds(qoff, qb), 0:LANES] = c0
        o_ref[pl.ds(qoff, qb), LANES:2 * LANES] = c1
        if want_lse:
            lse = jnp.broadcast_to(m + jnp.log(den), (Q_HEADS * qb, LANES))
            l0, l1 = _unstack_heads(lse, qb, lo)
            lse_ref[pl.ds(qoff, qb), 0:LANES] = l0
            lse_ref[pl.ds(qoff, qb), LANES:2 * LANES] = l1
        return carry

    lax.fori_loop(0, nsub, body, 0)


def _local_attention(q, k, v, tab, sink_tab, *, batch, seq, mixer, dil, qb, kw, radius,
                     nbr=False, want_lse=False):
    length = seq // dil
    tq = min(length, 512)
    q3 = q.reshape(batch, length, dil * N_MIXERS * Q_SLOTS)
    k3 = k.reshape(batch, length, dil * KV_WIDTH)
    v3 = v.reshape(batch, length, dil * KV_WIDTH)
    has_sink = sink_tab is not None
    in_specs = [
        pl.BlockSpec((None, tq, Q_SLOTS), lambda b, j, i: (b, i, j * N_MIXERS + mixer)),
        pl.BlockSpec((None, length, MIXER_KV), lambda b, j, i: (b, 0, j * N_MIXERS + mixer)),
        pl.BlockSpec((None, length, MIXER_KV), lambda b, j, i: (b, 0, j * N_MIXERS + mixer)),
        _resident(tab.shape),
    ]
    args = [q3, k3, v3, tab]
    if has_sink:
        in_specs.append(_resident(sink_tab.shape))
        args.append(sink_tab)
    o_spec = pl.BlockSpec((None, tq, MIXER_Q), lambda b, j, i: (b, i, j))
    o_shape = jax.ShapeDtypeStruct((batch, length, dil * MIXER_Q), F32)
    kern = functools.partial(_local_kernel, qb=qb, kw=kw, tq=tq, length=length, radius=radius,
                             nbr=nbr, has_sink=has_sink, want_lse=want_lse)
    outs = pl.pallas_call(
        kern,
        grid=(batch, dil, length // tq),
        in_specs=in_specs,
        out_specs=[o_spec, o_spec] if want_lse else [o_spec],
        out_shape=[o_shape, o_shape] if want_lse else [o_shape],
        compiler_params=_cparams(("parallel", "parallel", "arbitrary")),
        name=f"local_attn_m{mixer}_d{dil}",
    )(*args)
    return [t.reshape(batch * seq, MIXER_Q) for t in outs]


def _dense_kernel(q_ref, k_ref, v_ref, o_ref, *, tq, tk, seq):
    qst = _stack_heads(q_ref[...])
    rows = Q_HEADS * tq

    def body(j, carry):
        m, l, acc = carry
        koff = pl.multiple_of(j * tk, tk)
        kb = k_ref[pl.ds(koff, tk), :]
        vb = v_ref[pl.ds(koff, tk), :]
        s = lax.dot_general(qst, kb, (((1,), (1,)), ((), ())), preferred_element_type=F32)
        m_new = jnp.maximum(m, jnp.max(s, axis=-1, keepdims=True))
        alpha = jnp.exp(m - m_new)
        p = jnp.exp(s - m_new)
        l = alpha * l + jnp.sum(p, axis=-1, keepdims=True)
        acc = alpha * acc + jnp.dot(p.astype(BF16), vb, preferred_element_type=F32)
        return m_new, l, acc

    init = (jnp.full((rows, 1), NEG_INF, F32), jnp.zeros((rows, 1), F32), jnp.zeros((rows, LANES), F32))
    _, l, acc = lax.fori_loop(0, seq // tk, body, init)
    lo = lax.broadcasted_iota(jnp.int32, (1, LANES), 1) < HEAD_DIM
    c0, c1 = _unstack_heads(acc / l, tq, lo)
    o_ref[:, 0:LANES] = c0
    o_ref[:, LANES:2 * LANES] = c1


def _dense_attention(q, k, v, *, batch, seq, mixer, tq, tk):
    q3 = q.reshape(batch, seq, N_MIXERS * Q_SLOTS)
    k3 = k.reshape(batch, seq, KV_WIDTH)
    v3 = v.reshape(batch, seq, KV_WIDTH)
    out = pl.pallas_call(
        functools.partial(_dense_kernel, tq=tq, tk=tk, seq=seq),
        grid=(batch, seq // tq),
        in_specs=[
            pl.BlockSpec((None, tq, Q_SLOTS), lambda b, i: (b, i, mixer)),
            pl.BlockSpec((None, seq, MIXER_KV), lambda b, i: (b, 0, mixer)),
            pl.BlockSpec((None, seq, MIXER_KV), lambda b, i: (b, 0, mixer)),
        ],
        out_specs=pl.BlockSpec((None, tq, MIXER_Q), lambda b, i: (b, i, 0)),
        out_shape=jax.ShapeDtypeStruct((batch, seq, MIXER_Q), F32),
        compiler_params=_cparams(("parallel", "arbitrary")),
        name="dense_attn",
    )(q3, k3, v3)
    return out.reshape(batch * seq, MIXER_Q)


def _outproj_kernel(x_ref, oa_ref, ob_ref, oc1_ref, lc1_ref, oc2_ref, lc2_ref, oc3_ref, lc3_ref,
                    od_ref, gain_ref, w_ref, y_ref):
    l1, l2, l3 = lc1_ref[...], lc2_ref[...], lc3_ref[...]
    mx = jnp.maximum(jnp.maximum(l1, l2), l3)
    e1, e2, e3 = jnp.exp(l1 - mx), jnp.exp(l2 - mx), jnp.exp(l3 - mx)
    oc = (e1 * oc1_ref[...] + e2 * oc2_ref[...] + e3 * oc3_ref[...]) / (e1 + e2 + e3)
    acc = x_ref[...]
    for mixer, o in enumerate((oa_ref[...], ob_ref[...], oc, od_ref[...])):
        cols = slice(mixer * MIXER_Q, (mixer + 1) * MIXER_Q)
        ms = jnp.mean(o * o, axis=-1, keepdims=True)
        on = (o * lax.rsqrt(ms + EPS) * gain_ref[:, cols]).astype(BF16)
        acc = acc + jnp.dot(on, w_ref[cols, :], preferred_element_type=F32)
    y_ref[...] = acc


def _out_proj(x2, o_parts, gain, w, tm):
    n = x2.shape[0]
    row_spec = pl.BlockSpec((tm, D_MODEL), lambda i: (i, 0))
    part_spec = pl.BlockSpec((tm, MIXER_Q), lambda i: (i, 0))
    return pl.pallas_call(
        _outproj_kernel,
        grid=(n // tm,),
        in_specs=[row_spec] + [part_spec] * len(o_parts) + [_resident((1, MIX_WIDTH)),
                                                           _resident((MIX_WIDTH, D_MODEL))],
        out_specs=row_spec,
        out_shape=jax.ShapeDtypeStruct((n, D_MODEL), F32),
        compiler_params=_cparams(("parallel",)),
        name="out_proj",
    )(x2, *o_parts, gain, w)


def _ffn_kernel(x_ref, xp_ref, xn_ref, nw_ref, wg_ref, wv_ref, cw_ref, cb_ref, wd_ref, y_ref,
                h_ref, acc_ref, *, tm, tiles_per_seq):
    i = pl.program_id(0)
    first = (i % tiles_per_seq) == 0
    last = (i % tiles_per_seq) == tiles_per_seq - 1
    x = x_ref[...]
    xp = jnp.where(first, 0.0, xp_ref[...])
    xn = jnp.where(last, 0.0, xn_ref[...])
    xe = jnp.concatenate([xp, x, xn], axis=0)
    ms = jnp.mean(xe * xe, axis=-1, keepdims=True)
    h_ref[...] = (xe * lax.rsqrt(ms + EPS) * nw_ref[...]).astype(BF16)
    acc_ref[...] = x
    ext = tm + 2 * HALO

    def body(f, carry):
        h = h_ref[...]
        g = jnp.dot(h, wg_ref[f], preferred_element_type=F32)
        u = jnp.dot(h, wv_ref[f], preferred_element_type=F32)
        cw = cw_ref[f]
        g_prev = pltpu.roll(g, 1, 0)
        g_next = pltpu.roll(g, ext - 1, 0)
        gc = cw[0:1] * g_prev + cw[1:2] * g + cw[2:3] * g_next + cb_ref[f]
        gc = gc[HALO:HALO + tm]
        act = 0.5 * gc * (1.0 + lax.erf(gc * (2.0 ** -0.5)))
        yv = (act * u[HALO:HALO + tm]).astype(BF16)
        acc_ref[...] += jnp.dot(yv, wd_ref[f], preferred_element_type=F32)
        return carry

    lax.fori_loop(0, N_FF_CHUNKS, body, 0)
    y_ref[...] = acc_ref[...]


def _ffn(x2, nw, wg, wv, cw, cb, wd, seq, tm):
    n = x2.shape[0]
    tiles_per_seq = seq // tm
    blocks_per_tile = tm // HALO
    n_halo_blocks = n // HALO
    return pl.pallas_call(
        functools.partial(_ffn_kernel, tm=tm, tiles_per_seq=tiles_per_seq),
        grid=(n // tm,),
        in_specs=[
            pl.BlockSpec((tm, D_MODEL), lambda i: (i, 0)),
            pl.BlockSpec((HALO, D_MODEL), lambda i: (jnp.maximum(i * blocks_per_tile - 1, 0), 0)),
            pl.BlockSpec((HALO, D_MODEL),
                         lambda i: (jnp.minimum((i + 1) * blocks_per_tile, n_halo_blocks - 1), 0)),
            _resident((1, D_MODEL)),
            _resident((N_FF_CHUNKS, D_MODEL, FF_CHUNK)),
            _resident((N_FF_CHUNKS, D_MODEL, FF_CHUNK)),
            _resident((N_FF_CHUNKS, 3, FF_CHUNK)),
            _resident((N_FF_CHUNKS, 1, FF_CHUNK)),
            _resident((N_FF_CHUNKS, FF_CHUNK, D_MODEL)),
        ],
        out_specs=pl.BlockSpec((tm, D_MODEL), lambda i: (i, 0)),
        out_shape=jax.ShapeDtypeStruct((n, D_MODEL), F32),
        scratch_shapes=[pltpu.VMEM((tm + 2 * HALO, D_MODEL), BF16), pltpu.VMEM((tm, D_MODEL), F32)],
        compiler_params=_cparams(("parallel",)),
        name="conv_glu_ffn",
    )(x2, x2, x2, nw, wg, wv, cw, cb, wd)


def _alibi_slopes():
    n = 2 * Q_HEADS
    s = 2.0 ** (-8.0 * np.arange(1, n + 1, dtype=np.float64) / n)
    return s[0::2], s[1::2]


def _band_table(radius, qb, kw, slopes, step):
    qi = np.arange(qb)[:, None]
    ki = np.arange(kw)[None, :]
    out = np.empty((3, Q_HEADS, qb, kw), np.float32)
    for var, off in enumerate((0, radius, 2 * radius)):
        dist = np.abs(qi + off - ki)
        for h in range(Q_HEADS):
            out[var, h] = np.where(dist <= radius, -slopes[h] * step * dist, NEG_INF)
    return jnp.asarray(out.reshape(3, Q_HEADS * qb, kw))


def _nbr_table(rpb):
    var = np.arange(NA_ROWS)[:, None]
    j = np.arange(NA_ROWS)[None, :]
    dr = j - var + (NA_ROWS - 1)
    qc = np.arange(GRID_W)[:, None]
    c = np.arange(GRID_W)[None, :]
    cstart = np.clip(qc - NA_COLS // 2, 0, GRID_W - NA_COLS)
    valid = (c >= cstart) & (c < cstart + NA_COLS)
    dc = np.clip(c - qc + (NA_COLS - 1), 0, 2 * NA_COLS - 2)
    t = rpb[:, dr[:, None, :, None], dc[None, :, None, :]]
    t = jnp.where(valid[None, None, :, None, :], t.astype(F32), NEG_INF)
    t = jnp.transpose(t, (1, 0, 2, 3, 4))
    return t.reshape(NA_ROWS, Q_HEADS * GRID_W, NA_ROWS * GRID_W)


def _rope_tables(seq):
    rot_row = HEAD_DIM // 2
    rot_col = HEAD_DIM - rot_row
    t = jnp.arange(seq)
    row = (t // GRID_W).astype(F32)
    col = (t % GRID_W).astype(F32)
    f_row = ROPE_THETA ** (-jnp.arange(0, rot_row, 2, dtype=F32) / rot_row)
    f_col = ROPE_THETA ** (-jnp.arange(0, rot_col, 2, dtype=F32) / rot_col)
    ang = jnp.concatenate([row[:, None] * f_row[None, :], col[:, None] * f_col[None, :]], axis=-1)
    cos, sin = jnp.cos(ang), jnp.sin(ang)
    cos_t = jnp.tile(jnp.concatenate([cos, cos], axis=-1), (1, LANES // HEAD_DIM))
    sin_t = jnp.tile(jnp.concatenate([-sin, sin], axis=-1), (1, LANES // HEAD_DIM))
    return cos_t, sin_t


def _layer_params(norm1_w, w_in, q_norm_w, k_norm_w, sink_a, rpb_b, out_norm_w, w_out,
                  norm2_w, w_gate, w_val, conv_w, conv_b, w_down):
    d_even_odd = np.concatenate([np.arange(0, HEAD_DIM, 2), np.arange(1, HEAD_DIM, 2)])
    d_plain = np.arange(HEAD_DIM)
    q_cols, q_gain_idx, o_rows = [], [], []
    for mixer in range(N_MIXERS):
        d_order = d_even_odd if mixer == N_MIXERS - 1 else d_plain
        for g in range(Q_HEADS // KV_HEADS):
            for kv in range(KV_HEADS):
                head = kv * (Q_HEADS // KV_HEADS) + g
                q_cols.append(mixer * MIXER_Q + head * HEAD_DIM + d_order)
                q_gain_idx.append(mixer * HEAD_DIM + d_order)
                o_rows.append(mixer * MIXER_Q + head * HEAD_DIM + d_plain)
    k_cols, k_gain_idx = [], []
    for mixer in range(N_MIXERS):
        d_order = d_even_odd if mixer == N_MIXERS - 1 else d_plain
        for kv in range(KV_HEADS):
            k_cols.append(MIX_WIDTH + mixer * MIXER_KV + kv * HEAD_DIM + d_order)
            k_gain_idx.append(mixer * HEAD_DIM + d_order)
    v_cols = np.arange(MIX_WIDTH + KV_WIDTH, IN_WIDTH)
    in_cols = np.concatenate(q_cols + k_cols + [v_cols])
    o_rows = np.concatenate(o_rows)
    scale = HEAD_DIM ** -0.5
    gain = jnp.concatenate([q_norm_w.reshape(-1)[np.concatenate(q_gain_idx)] * scale,
                            k_norm_w.reshape(-1)[np.concatenate(k_gain_idx)]])
    return dict(
        norm1_w=norm1_w.reshape(1, D_MODEL),
        w_in=w_in[:, in_cols].astype(BF16),
        qk_gain=gain.reshape(1, MIX_WIDTH + KV_WIDTH),
        sink=sink_a,
        rpb=rpb_b,
        out_gain=out_norm_w[o_rows].reshape(1, MIX_WIDTH),
        w_out=w_out[o_rows, :].astype(BF16),
        norm2_w=norm2_w.reshape(1, D_MODEL),
        w_gate=jnp.transpose(w_gate.reshape(D_MODEL, N_FF_CHUNKS, FF_CHUNK), (1, 0, 2)).astype(BF16),
        w_val=jnp.transpose(w_val.reshape(D_MODEL, N_FF_CHUNKS, FF_CHUNK), (1, 0, 2)).astype(BF16),
        conv_w=jnp.transpose(conv_w.reshape(3, N_FF_CHUNKS, FF_CHUNK), (1, 0, 2)),
        conv_b=conv_b.reshape(N_FF_CHUNKS, 1, FF_CHUNK),
        w_down=w_down.reshape(N_FF_CHUNKS, FF_CHUNK, D_MODEL).astype(BF16),
    )


BAND_A = dict(qb=128, kw=3 * 128, radius=WIN_A)
BAND_C = dict(qb=64, kw=3 * 64, radius=64)
ROW_TILE = 512
DENSE_TQ = 128
DENSE_TK = 2048


def _layer(x2, p, tabs, batch, seq):
    q, k, v = _qkv_proj(x2, p["norm1_w"], p["w_in"], p["qk_gain"], tabs["cos"], tabs["sin"], seq, ROW_TILE)
    sink_tab = jnp.broadcast_to(jnp.repeat(p["sink"], BAND_A["qb"])[:, None], (Q_HEADS * BAND_A["qb"], LANES))
    (o_a,) = _local_attention(q, k, v, tabs["band_a"], sink_tab, batch=batch, seq=seq, mixer=0, dil=1, **BAND_A)
    (o_b,) = _local_attention(q, k, v, _nbr_table(p["rpb"]), None, batch=batch, seq=seq, mixer=1, dil=1,
                              qb=GRID_W, kw=NA_ROWS * GRID_W, radius=0, nbr=True)
    c_parts = []
    for (_, dil), tab in zip(DILATIONS, tabs["band_c"]):
        c_parts += _local_attention(q, k, v, tab, None, batch=batch, seq=seq, mixer=2, dil=dil,
                                    want_lse=True, **BAND_C)
    o_d = _dense_attention(q, k, v, batch=batch, seq=seq, mixer=3, tq=DENSE_TQ, tk=min(seq, DENSE_TK))
    x2 = _out_proj(x2, [o_a, o_b] + c_parts + [o_d], p["out_gain"], p["w_out"], ROW_TILE)
    return _ffn(x2, p["norm2_w"], p["w_gate"], p["w_val"], p["conv_w"], p["conv_b"], p["w_down"], seq, ROW_TILE)


def kernel(x_prompt, x_sample, norm1_w, w_in, q_norm_w, k_norm_w, sink_a, rpb_b, out_norm_w, w_out,
           norm2_w, w_gate, w_val, conv_w, conv_b, w_down):
    depth = norm1_w.shape[0]
    slopes_a, slopes_c = _alibi_slopes()
    band_a = _band_table(BAND_A["radius"], BAND_A["qb"], BAND_A["kw"], slopes_a, 1)
    band_c = [_band_table(BAND_C["radius"], BAND_C["qb"], BAND_C["kw"], slopes_c, dil) for _, dil in DILATIONS]
    layers = [_layer_params(norm1_w[l], w_in[l], q_norm_w[l], k_norm_w[l], sink_a[l], rpb_b[l],
                            out_norm_w[l], w_out[l], norm2_w[l], w_gate[l], w_val[l], conv_w[l],
                            conv_b[l], w_down[l]) for l in range(depth)]
    outs = []
    for x in (x_prompt, x_sample):
        batch, seq, _ = x.shape
        cos, sin = _rope_tables(seq)
        tabs = dict(cos=cos, sin=sin, band_a=band_a, band_c=band_c)
        x2 = x.reshape(batch * seq, D_MODEL)
        for p in layers:
            x2 = _layer(x2, p, tabs, batch, seq)
        outs.append(x2.reshape(batch, seq, D_MODEL))
    return tuple(outs)
```

```python
import functools
import math

import numpy as np
import jax
import jax.numpy as jnp
from jax import lax
from jax.experimental import pallas as pl
from jax.experimental.pallas import tpu as pltpu

D_MODEL = 1024
N_MIXERS = 4
Q_HEADS = 4
KV_HEADS = 2
HEAD_DIM = 64
MIX_WIDTH = N_MIXERS * Q_HEADS * HEAD_DIM
KV_WIDTH = N_MIXERS * KV_HEADS * HEAD_DIM
IN_WIDTH = MIX_WIDTH + 2 * KV_WIDTH
MIXER_Q = Q_HEADS * HEAD_DIM
MIXER_KV = KV_HEADS * HEAD_DIM
D_FF = 2816
GRID_W = 64
WIN_A = 128
NA_ROWS = 8
NA_COLS = 16
DILATIONS = ((128, 1), (512, 4), (2048, 16))
ROPE_THETA = 10000.0
EPS = 1e-6
NEG_INF = -1e30
LOG2E = math.log2(math.e)

LANES = 128
Q_SLOTS = Q_HEADS * LANES
FF_CHUNK = 256
N_FF_CHUNKS = D_FF // FF_CHUNK
HALO = 8
VMEM_LIMIT = 56 * 1024 * 1024

MIXER_C = 2
MIXER_D = 3
PLANE_DILS = tuple(d for _, d in DILATIONS if d > 1)

F32 = jnp.float32
BF16 = jnp.bfloat16


def _cparams(sem):
    return pltpu.CompilerParams(dimension_semantics=sem, vmem_limit_bytes=VMEM_LIMIT)


def _resident(shape):
    nd = len(shape)
    return pl.BlockSpec(shape, lambda *_: (0,) * nd, pipeline_mode=pl.Buffered(1))


N_Q_CHUNKS = MIX_WIDTH // LANES
N_QK_CHUNKS = (MIX_WIDTH + KV_WIDTH) // LANES
ROPE_CHUNKS = (2 * MIXER_D, 2 * MIXER_D + 1, N_Q_CHUNKS + MIXER_D)


def _qkv_kernel(x_ref, nw_ref, w_ref, gain_ref, cos_ref, sin_ref, q_ref, k_ref, v_ref, *rest, tm):
    plane_refs, perm_ref = rest[:-1], rest[-1]
    planes = {dil: plane_refs[3 * n:3 * n + 3] for n, dil in enumerate(PLANE_DILS)}
    x = x_ref[...]
    ms = jnp.mean(x * x, axis=-1, keepdims=True)
    h = (x * lax.rsqrt(ms + EPS) * nw_ref[...]).astype(BF16)
    y = jnp.dot(h, w_ref[...], preferred_element_type=F32)
    lane = lax.broadcasted_iota(jnp.int32, (1, LANES), 1)
    lo = lane < HEAD_DIM
    first_half = (lane % HEAD_DIM) < HEAD_DIM // 2
    cos = cos_ref[...]
    sin = sin_ref[...]

    def q_slots(c):
        return jnp.where(lo, c, 0.0).astype(BF16), jnp.where(lo, 0.0, c).astype(BF16)

    def to_planes(slot, c, store):
        perm_ref[slot] = c
        for dil in PLANE_DILS:
            for j in range(dil):
                store(planes[dil], j, perm_ref[slot, pl.ds(j, tm // dil, stride=dil), :])

    for j in range(N_QK_CHUNKS):
        c = y[:, j * LANES:(j + 1) * LANES]
        c2 = c * c
        tot = jnp.sum(c2, axis=-1, keepdims=True)
        s_lo = jnp.sum(jnp.where(lo, c2, 0.0), axis=-1, keepdims=True)
        msq = jnp.where(lo, s_lo, tot - s_lo) * (1.0 / HEAD_DIM)
        cn = c * lax.rsqrt(msq + EPS) * gain_ref[:, j * LANES:(j + 1) * LANES]
        if j in ROPE_CHUNKS:
            partner = jnp.where(first_half,
                                pltpu.roll(cn, LANES - HEAD_DIM // 2, 1),
                                pltpu.roll(cn, HEAD_DIM // 2, 1))
            cn = cn * cos + partner * sin
        if j < N_Q_CHUNKS:
            mixer, g = divmod(j, 2)
            base = mixer * Q_SLOTS
            s0, s1 = q_slots(cn)
            q_ref[:, base + g * LANES:base + (g + 1) * LANES] = s0
            q_ref[:, base + (2 + g) * LANES:base + (3 + g) * LANES] = s1
            if mixer == MIXER_C:
                def store_q(refs, plane, part, g=g):
                    p0, p1 = q_slots(part)
                    refs[0][plane, :, g * LANES:(g + 1) * LANES] = p0
                    refs[0][plane, :, (2 + g) * LANES:(3 + g) * LANES] = p1
                to_planes(g, cn, store_q)
        else:
            mixer = j - N_Q_CHUNKS
            k_ref[:, mixer * LANES:(mixer + 1) * LANES] = cn.astype(BF16)
            if mixer == MIXER_C:
                def store_k(refs, plane, part):
                    refs[1][plane] = part.astype(BF16)
                to_planes(2, cn, store_k)
    vals = y[:, MIX_WIDTH + KV_WIDTH:]
    v_ref[...] = vals.astype(BF16)

    def store_v(refs, plane, part):
        refs[2][plane] = part.astype(BF16)
    to_planes(3, vals[:, MIXER_C * LANES:(MIXER_C + 1) * LANES], store_v)


def _qkv_proj(x2, nw, w, gain, cos, sin, batch, seq, tm):
    n = x2.shape[0]
    tiles_per_seq = seq // tm
    out_specs = [
        pl.BlockSpec((tm, N_MIXERS * Q_SLOTS), lambda i: (i, 0)),
        pl.BlockSpec((tm, KV_WIDTH), lambda i: (i, 0)),
        pl.BlockSpec((tm, KV_WIDTH), lambda i: (i, 0)),
    ]
    out_shape = [
        jax.ShapeDtypeStruct((n, N_MIXERS * Q_SLOTS), BF16),
        jax.ShapeDtypeStruct((n, KV_WIDTH), BF16),
        jax.ShapeDtypeStruct((n, KV_WIDTH), BF16),
    ]
    for dil in PLANE_DILS:
        for width in (Q_SLOTS, MIXER_KV, MIXER_KV):
            out_specs.append(pl.BlockSpec((None, dil, tm // dil, width),
                                          lambda i: (i // tiles_per_seq, 0, i % tiles_per_seq, 0)))
            out_shape.append(jax.ShapeDtypeStruct((batch, dil, seq // dil, width), BF16))
    return pl.pallas_call(
        functools.partial(_qkv_kernel, tm=tm),
        grid=(n // tm,),
        in_specs=[
            pl.BlockSpec((tm, D_MODEL), lambda i: (i, 0)),
            _resident((1, D_MODEL)),
            _resident((D_MODEL, IN_WIDTH)),
            _resident((1, MIX_WIDTH + KV_WIDTH)),
            pl.BlockSpec((tm, LANES), lambda i: (i % tiles_per_seq, 0)),
            pl.BlockSpec((tm, LANES), lambda i: (i % tiles_per_seq, 0)),
        ],
        out_specs=out_specs,
        out_shape=out_shape,
        scratch_shapes=[pltpu.VMEM((4, tm, LANES), F32)],
        compiler_params=_cparams(("parallel",)),
        name="qkv_proj",
    )(x2, nw, w, gain, cos, sin)


def _stack_heads(qs):
    return jnp.concatenate([qs[:, h * LANES:(h + 1) * LANES] for h in range(Q_HEADS)], axis=0)


def _unstack_heads(o, rows, lo):
    c0 = jnp.where(lo, o[0:rows], o[2 * rows:3 * rows])
    c1 = jnp.where(lo, o[rows:2 * rows], o[3 * rows:4 * rows])
    return c0, c1


LOCAL_TOKENS_PER_STEP = 1024
LOCAL_UNROLL = 4


def _local_kernel(*refs, dil, qb, kw, tq, length, radius, nbr, has_sink, want_lse):
    it = iter(refs)
    q_ref, k_ref, v_ref, tab_ref = next(it), next(it), next(it), next(it)
    sink_ref = next(it) if has_sink else None
    o_ref = next(it)
    lse_ref = next(it) if want_lse else None
    i = pl.program_id(1)
    nsub = tq // qb
    nblk = length // qb
    lo = lax.broadcasted_iota(jnp.int32, (1, LANES), 1) < HEAD_DIM

    def sub_block(f):
        plane, sb = (0, f) if dil == 1 else (f // nsub, f % nsub)
        blk = i * nsub + sb
        if nbr:
            rows = length // GRID_W
            r0 = jnp.clip(blk - NA_ROWS // 2, 0, rows - NA_ROWS)
            kstart = r0 * GRID_W
            var = blk - r0
        else:
            kstart = jnp.clip(blk * qb - radius, 0, length - kw)
            var = jnp.where(blk == 0, 0, jnp.where(blk == nblk - 1, 2, 1))
        kstart = pl.multiple_of(kstart, qb)
        qoff = pl.multiple_of(sb * qb, qb)
        qst = _stack_heads(q_ref[plane, pl.ds(qoff, qb), :])
        kb = k_ref[plane, pl.ds(kstart, kw), :]
        vb = v_ref[plane, pl.ds(kstart, kw), :]
        s = lax.dot_general(qst, kb, (((1,), (1,)), ((), ())), preferred_element_type=F32)
        s = s + tab_ref[var]
        n = Q_HEADS * qb
        m = jnp.broadcast_to(jnp.max(s, axis=-1, keepdims=True), (n, LANES))
        if has_sink:
            sk = sink_ref[...]
            m = jnp.maximum(m, sk)
        chunks = [jnp.exp(s[:, c * LANES:(c + 1) * LANES] - m) for c in range(kw // LANES)]
        if kw % LANES:
            chunks.append(jnp.exp(s[:, kw - kw % LANES:] - m[:, :kw % LANES]))
        p = jnp.concatenate(chunks, axis=-1)
        den = jnp.broadcast_to(jnp.sum(p, axis=-1, keepdims=True), (n, LANES))
        if has_sink:
            den = den + jnp.exp(sk - m)
        acc = jnp.dot(p.astype(BF16), vb, preferred_element_type=F32)
        out_rows = pl.ds(qoff, qb) if dil == 1 else pl.ds(qoff * dil + plane, qb, stride=dil)
        c0, c1 = _unstack_heads(acc / den, qb, lo)
        o_ref[0, out_rows, :] = c0
        o_ref[1, out_rows, :] = c1
        if want_lse:
            l0, l1 = _unstack_heads(m + jnp.log(den), qb, lo)
            lse_ref[0, out_rows, :] = l0
            lse_ref[1, out_rows, :] = l1

    def group(gi, carry):
        for u in range(LOCAL_UNROLL):
            sub_block(gi * LOCAL_UNROLL + u)
        return carry

    lax.fori_loop(0, dil * nsub // LOCAL_UNROLL, group, 0)


def _local_attention(q, k, v, tab, sink_tab, *, batch, seq, dil, col, qb, kw, radius, name,
                     nbr=False, want_lse=False):
    length = seq // dil
    tq = min(length, LOCAL_TOKENS_PER_STEP // dil)
    has_sink = sink_tab is not None
    kv_spec = pl.BlockSpec((None, dil, length, MIXER_KV), lambda b, i: (b, 0, 0, col))
    in_specs = [pl.BlockSpec((None, dil, tq, Q_SLOTS), lambda b, i: (b, 0, i, col)), kv_spec, kv_spec,
                _resident(tab.shape)]
    args = [q, k, v, tab]
    if has_sink:
        in_specs.append(_resident(sink_tab.shape))
        args.append(sink_tab)
    o_spec = pl.BlockSpec((None, MIXER_Q // LANES, tq * dil, LANES), lambda b, i: (b, 0, i, 0))
    o_shape = jax.ShapeDtypeStruct((batch, MIXER_Q // LANES, seq, LANES), F32)
    kern = functools.partial(_local_kernel, dil=dil, qb=qb, kw=kw, tq=tq, length=length, radius=radius,
                             nbr=nbr, has_sink=has_sink, want_lse=want_lse)
    outs = pl.pallas_call(
        kern,
        grid=(batch, length // tq),
        in_specs=in_specs,
        out_specs=[o_spec, o_spec] if want_lse else [o_spec],
        out_shape=[o_shape, o_shape] if want_lse else [o_shape],
        compiler_params=_cparams(("parallel", "arbitrary")),
        name=name,
    )(*args)
    return list(outs)


DENSE_CHAINS = 4


def _dense_kernel(q_ref, k_ref, v_ref, o_ref, *, tq, tk, seq):
    hq = tq // DENSE_CHAINS
    rows = Q_HEADS * hq
    qsts = [_stack_heads(q_ref[c * hq:(c + 1) * hq, :]) for c in range(DENSE_CHAINS)]

    def body(j, carry):
        koff = pl.multiple_of(j * tk, tk)
        kb = k_ref[pl.ds(koff, tk), :]
        vb = v_ref[pl.ds(koff, tk), :]
        out = []
        for c in range(DENSE_CHAINS):
            m, l, acc = carry[c]
            s = lax.dot_general(qsts[c], kb, (((1,), (1,)), ((), ())), preferred_element_type=F32)
            m_new = jnp.maximum(m, jnp.max(s, axis=-1, keepdims=True))
            alpha = jnp.exp2(m - m_new)
            p = jnp.exp2(s - m_new)
            l = alpha * l + jnp.sum(p, axis=-1, keepdims=True)
            acc = alpha * acc + jnp.dot(p.astype(BF16), vb, preferred_element_type=F32)
            out.append((m_new, l, acc))
        return tuple(out)

    init = tuple((jnp.full((rows, 1), NEG_INF, F32), jnp.zeros((rows, 1), F32), jnp.zeros((rows, LANES), F32))
                 for _ in range(DENSE_CHAINS))
    final = lax.fori_loop(0, seq // tk, body, init)
    lo = lax.broadcasted_iota(jnp.int32, (1, LANES), 1) < HEAD_DIM
    for c, (_, l, acc) in enumerate(final):
        c0, c1 = _unstack_heads(acc / l, hq, lo)
        o_ref[c * hq:(c + 1) * hq, 0:LANES] = c0
        o_ref[c * hq:(c + 1) * hq, LANES:2 * LANES] = c1


def _dense_attention(q, k, v, *, batch, seq, mixer, tq, tk):
    q3 = q.reshape(batch, seq, N_MIXERS * Q_SLOTS)
    k3 = k.reshape(batch, seq, KV_WIDTH)
    v3 = v.reshape(batch, seq, KV_WIDTH)
    out = pl.pallas_call(
        functools.partial(_dense_kernel, tq=tq, tk=tk, seq=seq),
        grid=(batch, seq // tq),
        in_specs=[
            pl.BlockSpec((None, tq, Q_SLOTS), lambda b, i: (b, i, mixer)),
            pl.BlockSpec((None, seq, MIXER_KV), lambda b, i: (b, 0, mixer)),
            pl.BlockSpec((None, seq, MIXER_KV), lambda b, i: (b, 0, mixer)),
        ],
        out_specs=pl.BlockSpec((None, tq, MIXER_Q), lambda b, i: (b, i, 0)),
        out_shape=jax.ShapeDtypeStruct((batch, seq, MIXER_Q), F32),
        compiler_params=_cparams(("parallel", "arbitrary")),
        name="dense_attn",
    )(q3, k3, v3)
    return out.reshape(batch * seq, MIXER_Q)


def _outproj_kernel(x_ref, oa_ref, ob_ref, oc1_ref, lc1_ref, oc2_ref, lc2_ref, oc3_ref, lc3_ref,
                    od_ref, gain_ref, w_ref, y_ref):
    def halves(ref):
        return jnp.concatenate([ref[0], ref[1]], axis=-1)

    l1, l2, l3 = halves(lc1_ref), halves(lc2_ref), halves(lc3_ref)
    mx = jnp.maximum(jnp.maximum(l1, l2), l3)
    e1, e2, e3 = jnp.exp(l1 - mx), jnp.exp(l2 - mx), jnp.exp(l3 - mx)
    oc = (e1 * halves(oc1_ref) + e2 * halves(oc2_ref) + e3 * halves(oc3_ref)) / (e1 + e2 + e3)
    acc = x_ref[...]
    for mixer, o in enumerate((halves(oa_ref), halves(ob_ref), oc, od_ref[...])):
        cols = slice(mixer * MIXER_Q, (mixer + 1) * MIXER_Q)
        ms = jnp.mean(o * o, axis=-1, keepdims=True)
        on = (o * lax.rsqrt(ms + EPS) * gain_ref[:, cols]).astype(BF16)
        acc = acc + jnp.dot(on, w_ref[cols, :], preferred_element_type=F32)
    y_ref[...] = acc


def _out_proj(x2, local_parts, o_dense, gain, w, seq, tm):
    n = x2.shape[0]
    tiles_per_seq = seq // tm
    row_spec = pl.BlockSpec((tm, D_MODEL), lambda i: (i, 0))
    local_spec = pl.BlockSpec((None, MIXER_Q // LANES, tm, LANES),
                              lambda i: (i // tiles_per_seq, 0, i % tiles_per_seq, 0))
    dense_spec = pl.BlockSpec((tm, MIXER_Q), lambda i: (i, 0))
    return pl.pallas_call(
        _outproj_kernel,
        grid=(n // tm,),
        in_specs=[row_spec] + [local_spec] * len(local_parts) + [dense_spec, _resident((1, MIX_WIDTH)),
                                                                 _resident((MIX_WIDTH, D_MODEL))],
        out_specs=row_spec,
        out_shape=jax.ShapeDtypeStruct((n, D_MODEL), F32),
        compiler_params=_cparams(("parallel",)),
        name="out_proj",
    )(x2, *local_parts, o_dense, gain, w)


def _ffn_kernel(x_ref, xp_ref, xn_ref, nw_ref, wg_ref, wv_ref, cw_ref, cb_ref, wd_ref, y_ref,
                h_ref, acc_ref, *, tm, tiles_per_seq):
    i = pl.program_id(0)
    first = (i % tiles_per_seq) == 0
    last = (i % tiles_per_seq) == tiles_per_seq - 1
    x = x_ref[...]
    xp = jnp.where(first, 0.0, xp_ref[...])
    xn = jnp.where(last, 0.0, xn_ref[...])
    xe = jnp.concatenate([xp, x, xn], axis=0)
    ms = jnp.mean(xe * xe, axis=-1, keepdims=True)
    h_ref[...] = (xe * lax.rsqrt(ms + EPS) * nw_ref[...]).astype(BF16)
    ext = tm + 2 * HALO

    for f in range(N_FF_CHUNKS):
        h = h_ref[...]
        g = jnp.dot(h, wg_ref[f], preferred_element_type=F32)
        u = jnp.dot(h, wv_ref[f], preferred_element_type=F32)
        cw = cw_ref[f]
        g_prev = pltpu.roll(g, 1, 0)
        g_next = pltpu.roll(g, ext - 1, 0)
        gc = cw[0:1] * g_prev + cw[1:2] * g + cw[2:3] * g_next + cb_ref[f]
        gc = gc[HALO:HALO + tm]
        act = 0.5 * gc * (1.0 + lax.erf(gc * (2.0 ** -0.5)))
        yv = (act * u[HALO:HALO + tm]).astype(BF16)
        d = jnp.dot(yv, wd_ref[f], preferred_element_type=F32)
        if f == 0:
            acc_ref[...] = x + d
        elif f < N_FF_CHUNKS - 1:
            acc_ref[...] += d
        else:
            y_ref[...] = acc_ref[...] + d


def _ffn(x2, nw, wg, wv, cw, cb, wd, seq, tm):
    n = x2.shape[0]
    tiles_per_seq = seq // tm
    blocks_per_tile = tm // HALO
    n_halo_blocks = n // HALO
    return pl.pallas_call(
        functools.partial(_ffn_kernel, tm=tm, tiles_per_seq=tiles_per_seq),
        grid=(n // tm,),
        in_specs=[
            pl.BlockSpec((tm, D_MODEL), lambda i: (i, 0)),
            pl.BlockSpec((HALO, D_MODEL), lambda i: (jnp.maximum(i * blocks_per_tile - 1, 0), 0)),
            pl.BlockSpec((HALO, D_MODEL),
                         lambda i: (jnp.minimum((i + 1) * blocks_per_tile, n_halo_blocks - 1), 0)),
            _resident((1, D_MODEL)),
            _resident((N_FF_CHUNKS, D_MODEL, FF_CHUNK)),
            _resident((N_FF_CHUNKS, D_MODEL, FF_CHUNK)),
            _resident((N_FF_CHUNKS, 3, FF_CHUNK)),
            _resident((N_FF_CHUNKS, 1, FF_CHUNK)),
            _resident((N_FF_CHUNKS, FF_CHUNK, D_MODEL)),
        ],
        out_specs=pl.BlockSpec((tm, D_MODEL), lambda i: (i, 0)),
        out_shape=jax.ShapeDtypeStruct((n, D_MODEL), F32),
        scratch_shapes=[pltpu.VMEM((tm + 2 * HALO, D_MODEL), BF16), pltpu.VMEM((tm, D_MODEL), F32)],
        compiler_params=_cparams(("parallel",)),
        name="conv_glu_ffn",
    )(x2, x2, x2, nw, wg, wv, cw, cb, wd)


def _alibi_slopes():
    n = 2 * Q_HEADS
    s = 2.0 ** (-8.0 * np.arange(1, n + 1, dtype=np.float64) / n)
    return s[0::2], s[1::2]


def _band_table(radius, qb, kw, slopes, step):
    qi = np.arange(qb)[:, None]
    ki = np.arange(kw)[None, :]
    out = np.empty((3, Q_HEADS, qb, kw), np.float32)
    for var, off in enumerate((0, radius, 2 * radius)):
        dist = np.abs(qi + off - ki)
        for h in range(Q_HEADS):
            out[var, h] = np.where(dist <= radius, -slopes[h] * step * dist, NEG_INF)
    return jnp.asarray(out.reshape(3, Q_HEADS * qb, kw))


def _nbr_table(rpb):
    var = np.arange(NA_ROWS)[:, None]
    j = np.arange(NA_ROWS)[None, :]
    dr = j - var + (NA_ROWS - 1)
    qc = np.arange(GRID_W)[:, None]
    c = np.arange(GRID_W)[None, :]
    cstart = np.clip(qc - NA_COLS // 2, 0, GRID_W - NA_COLS)
    valid = (c >= cstart) & (c < cstart + NA_COLS)
    dc = np.clip(c - qc + (NA_COLS - 1), 0, 2 * NA_COLS - 2)
    t = rpb[:, dr[:, None, :, None], dc[None, :, None, :]]
    t = jnp.where(valid[None, None, :, None, :], t.astype(F32), NEG_INF)
    t = jnp.transpose(t, (1, 0, 2, 3, 4))
    return t.reshape(NA_ROWS, Q_HEADS * GRID_W, NA_ROWS * GRID_W)


def _rope_tables(seq):
    rot_row = HEAD_DIM // 2
    rot_col = HEAD_DIM - rot_row
    t = jnp.arange(seq)
    row = (t // GRID_W).astype(F32)
    col = (t % GRID_W).astype(F32)
    f_row = ROPE_THETA ** (-jnp.arange(0, rot_row, 2, dtype=F32) / rot_row)
    f_col = ROPE_THETA ** (-jnp.arange(0, rot_col, 2, dtype=F32) / rot_col)
    ang = jnp.concatenate([row[:, None] * f_row[None, :], col[:, None] * f_col[None, :]], axis=-1)
    cos, sin = jnp.cos(ang), jnp.sin(ang)
    cos_t = jnp.tile(jnp.concatenate([cos, cos], axis=-1), (1, LANES // HEAD_DIM))
    sin_t = jnp.tile(jnp.concatenate([-sin, sin], axis=-1), (1, LANES // HEAD_DIM))
    return cos_t, sin_t


def _even_odd(a):
    return jnp.concatenate([a[..., 0::2], a[..., 1::2]], axis=-1)


def _rotary_last_mixer(a, axis):
    head, tail = lax.slice_in_dim(a, 0, N_MIXERS - 1, axis=axis), lax.slice_in_dim(a, N_MIXERS - 1, N_MIXERS, axis=axis)
    return jnp.concatenate([head, _even_odd(tail)], axis=axis)


def _layer_params(norm1_w, w_in, q_norm_w, k_norm_w, sink_a, rpb_b, out_norm_w, w_out,
                  norm2_w, w_gate, w_val, conv_w, conv_b, w_down):
    groups = Q_HEADS // KV_HEADS
    wq = w_in[:, :MIX_WIDTH].reshape(D_MODEL, N_MIXERS, KV_HEADS, groups, HEAD_DIM)
    wq = _rotary_last_mixer(jnp.swapaxes(wq, 2, 3), 1).reshape(D_MODEL, MIX_WIDTH)
    wk = w_in[:, MIX_WIDTH:MIX_WIDTH + KV_WIDTH].reshape(D_MODEL, N_MIXERS, KV_HEADS, HEAD_DIM)
    wk = _rotary_last_mixer(wk, 1).reshape(D_MODEL, KV_WIDTH)
    w_in_p = jnp.concatenate([wq, wk, w_in[:, MIX_WIDTH + KV_WIDTH:]], axis=1).astype(BF16)
    q_scale = np.full((N_MIXERS, 1), HEAD_DIM ** -0.5, np.float32)
    q_scale[MIXER_D] *= LOG2E
    qg = _rotary_last_mixer(q_norm_w, 0) * q_scale
    kg = _rotary_last_mixer(k_norm_w, 0)
    gain = jnp.concatenate([jnp.broadcast_to(qg[:, None, :], (N_MIXERS, Q_HEADS, HEAD_DIM)).reshape(-1),
                            jnp.broadcast_to(kg[:, None, :], (N_MIXERS, KV_HEADS, HEAD_DIM)).reshape(-1)])
    out_gain = jnp.swapaxes(out_norm_w.reshape(N_MIXERS, KV_HEADS, groups, HEAD_DIM), 1, 2)
    w_out_p = jnp.swapaxes(w_out.reshape(N_MIXERS, KV_HEADS, groups, HEAD_DIM, D_MODEL), 1, 2)
    return dict(
        norm1_w=norm1_w.reshape(1, D_MODEL),
        w_in=w_in_p,
        qk_gain=gain.reshape(1, MIX_WIDTH + KV_WIDTH),
        sink=sink_a,
        rpb=rpb_b,
        out_gain=out_gain.reshape(1, MIX_WIDTH),
        w_out=w_out_p.reshape(MIX_WIDTH, D_MODEL).astype(BF16),
        norm2_w=norm2_w.reshape(1, D_MODEL),
        w_gate=jnp.transpose(w_gate.reshape(D_MODEL, N_FF_CHUNKS, FF_CHUNK), (1, 0, 2)).astype(BF16),
        w_val=jnp.transpose(w_val.reshape(D_MODEL, N_FF_CHUNKS, FF_CHUNK), (1, 0, 2)).astype(BF16),
        conv_w=jnp.transpose(conv_w.reshape(3, N_FF_CHUNKS, FF_CHUNK), (1, 0, 2)),
        conv_b=conv_b.reshape(N_FF_CHUNKS, 1, FF_CHUNK),
        w_down=w_down.reshape(N_FF_CHUNKS, FF_CHUNK, D_MODEL).astype(BF16),
    )


BAND_A = dict(qb=128, kw=3 * 128, radius=WIN_A)
BAND_C = dict(qb=64, kw=3 * 64, radius=64)
ROW_TILE = 512
DENSE_TQ = 512
DENSE_TK = 2048


def _layer(x2, p, tabs, batch, seq):
    q, k, v, *planes = _qkv_proj(x2, p["norm1_w"], p["w_in"], p["qk_gain"], tabs["cos"], tabs["sin"],
                                 batch, seq, ROW_TILE)
    q4 = q.reshape(batch, 1, seq, N_MIXERS * Q_SLOTS)
    k4 = k.reshape(batch, 1, seq, KV_WIDTH)
    v4 = v.reshape(batch, 1, seq, KV_WIDTH)
    sink_tab = jnp.broadcast_to(jnp.repeat(p["sink"], BAND_A["qb"])[:, None], (Q_HEADS * BAND_A["qb"], LANES))
    (o_a,) = _local_attention(q4, k4, v4, tabs["band_a"], sink_tab, batch=batch, seq=seq, dil=1, col=0,
                              name="attn_a", **BAND_A)
    (o_b,) = _local_attention(q4, k4, v4, _nbr_table(p["rpb"]), None, batch=batch, seq=seq, dil=1, col=1,
                              qb=GRID_W, kw=NA_ROWS * GRID_W, radius=0, nbr=True, name="attn_b")
    c_parts = _local_attention(q4, k4, v4, tabs["band_c"][0], None, batch=batch, seq=seq, dil=1, col=MIXER_C,
                               want_lse=True, name="attn_c1", **BAND_C)
    for n, dil in enumerate(PLANE_DILS):
        qd, kd, vd = planes[3 * n:3 * n + 3]
        c_parts += _local_attention(qd, kd, vd, tabs["band_c"][n + 1], None, batch=batch, seq=seq, dil=dil,
                                    col=0, want_lse=True, name=f"attn_c{dil}", **BAND_C)
    o_d = _dense_attention(q, k, v, batch=batch, seq=seq, mixer=MIXER_D, tq=DENSE_TQ, tk=min(seq, DENSE_TK))
    x2 = _out_proj(x2, [o_a, o_b] + c_parts, o_d, p["out_gain"], p["w_out"], seq, ROW_TILE)
    return _ffn(x2, p["norm2_w"], p["w_gate"], p["w_val"], p["conv_w"], p["conv_b"], p["w_down"], seq, ROW_TILE)


def kernel(x_prompt, x_sample, norm1_w, w_in, q_norm_w, k_norm_w, sink_a, rpb_b, out_norm_w, w_out,
           norm2_w, w_gate, w_val, conv_w, conv_b, w_down):
    depth = norm1_w.shape[0]
    slopes_a, slopes_c = _alibi_slopes()
    band_a = _band_table(BAND_A["radius"], BAND_A["qb"], BAND_A["kw"], slopes_a, 1)
    band_c = [_band_table(BAND_C["radius"], BAND_C["qb"], BAND_C["kw"], slopes_c, dil) for _, dil in DILATIONS]
    layers = [_layer_params(norm1_w[l], w_in[l], q_norm_w[l], k_norm_w[l], sink_a[l], rpb_b[l],
                            out_norm_w[l], w_out[l], norm2_w[l], w_gate[l], w_val[l], conv_w[l],
                            conv_b[l], w_down[l]) for l in range(depth)]
    outs = []
    for x in (x_prompt, x_sample):
        batch, seq, _ = x.shape
        cos, sin = _rope_tables(seq)
        tabs = dict(cos=cos, sin=sin, band_a=band_a, band_c=band_c)
        x2 = x.reshape(batch * seq, D_MODEL)
        for p in layers:
            x2 = _layer(x2, p, tabs, batch, seq)
        outs.append(x2.reshape(batch, seq, D_MODEL))
    return tuple(outs)
```

```python
import functools
import math

import numpy as np
import jax
import jax.numpy as jnp
from jax import lax
from jax.experimental import pallas as pl
from jax.experimental.pallas import tpu as pltpu

D_MODEL = 1024
N_MIXERS = 4
Q_HEADS = 4
KV_HEADS = 2
HEAD_DIM = 64
MIX_WIDTH = N_MIXERS * Q_HEADS * HEAD_DIM
KV_WIDTH = N_MIXERS * KV_HEADS * HEAD_DIM
IN_WIDTH = MIX_WIDTH + 2 * KV_WIDTH
MIXER_Q = Q_HEADS * HEAD_DIM
MIXER_KV = KV_HEADS * HEAD_DIM
D_FF = 2816
GRID_W = 64
WIN_A = 128
NA_ROWS = 8
NA_COLS = 16
DILATIONS = ((128, 1), (512, 4), (2048, 16))
ROPE_THETA = 10000.0
EPS = 1e-6
NEG_INF = -1e30
LOG2E = math.log2(math.e)

LANES = 128
Q_SLOTS = Q_HEADS * LANES
FF_CHUNK = 256
N_FF_CHUNKS = D_FF // FF_CHUNK
HALO = 8
VMEM_LIMIT = 56 * 1024 * 1024

MIXER_C = 2
MIXER_D = 3
PLANE_DILS = tuple(d for _, d in DILATIONS if d > 1)

F32 = jnp.float32
BF16 = jnp.bfloat16


def _cparams(sem):
    return pltpu.CompilerParams(dimension_semantics=sem, vmem_limit_bytes=VMEM_LIMIT)


def _resident(shape):
    nd = len(shape)
    return pl.BlockSpec(shape, lambda *_: (0,) * nd, pipeline_mode=pl.Buffered(1))


N_Q_CHUNKS = MIX_WIDTH // LANES
N_QK_CHUNKS = (MIX_WIDTH + KV_WIDTH) // LANES
ROPE_CHUNKS = (2 * MIXER_D, 2 * MIXER_D + 1, N_Q_CHUNKS + MIXER_D)


def _qkv_kernel(x_ref, nw_ref, w_ref, gain_ref, cos_ref, sin_ref, q_ref, k_ref, v_ref, *rest, tm):
    plane_refs, perm_ref = rest[:-1], rest[-1]
    planes = {dil: plane_refs[3 * n:3 * n + 3] for n, dil in enumerate(PLANE_DILS)}
    x = x_ref[...]
    ms = jnp.mean(x * x, axis=-1, keepdims=True)
    h = (x * lax.rsqrt(ms + EPS) * nw_ref[...]).astype(BF16)
    y = jnp.dot(h, w_ref[...], preferred_element_type=F32)
    lane = lax.broadcasted_iota(jnp.int32, (1, LANES), 1)
    lo = lane < HEAD_DIM
    first_half = (lane % HEAD_DIM) < HEAD_DIM // 2
    cos = cos_ref[...]
    sin = sin_ref[...]

    def q_slots(c):
        return jnp.where(lo, c, 0.0).astype(BF16), jnp.where(lo, 0.0, c).astype(BF16)

    def to_planes(slot, c, store):
        perm_ref[slot] = c
        for dil in PLANE_DILS:
            for j in range(dil):
                store(planes[dil], j, perm_ref[slot, pl.ds(j, tm // dil, stride=dil), :])

    for j in range(N_QK_CHUNKS):
        c = y[:, j * LANES:(j + 1) * LANES]
        c2 = c * c
        tot = jnp.sum(c2, axis=-1, keepdims=True)
        s_lo = jnp.sum(jnp.where(lo, c2, 0.0), axis=-1, keepdims=True)
        msq = jnp.where(lo, s_lo, tot - s_lo) * (1.0 / HEAD_DIM)
        cn = c * lax.rsqrt(msq + EPS) * gain_ref[:, j * LANES:(j + 1) * LANES]
        if j in ROPE_CHUNKS:
            partner = jnp.where(first_half,
                                pltpu.roll(cn, LANES - HEAD_DIM // 2, 1),
                                pltpu.roll(cn, HEAD_DIM // 2, 1))
            cn = cn * cos + partner * sin
        if j < N_Q_CHUNKS:
            mixer, g = divmod(j, 2)
            base = mixer * Q_SLOTS
            s0, s1 = q_slots(cn)
            q_ref[:, base + g * LANES:base + (g + 1) * LANES] = s0
            q_ref[:, base + (2 + g) * LANES:base + (3 + g) * LANES] = s1
            if mixer == MIXER_C:
                def store_q(refs, plane, part, g=g):
                    p0, p1 = q_slots(part)
                    refs[0][plane, :, g * LANES:(g + 1) * LANES] = p0
                    refs[0][plane, :, (2 + g) * LANES:(3 + g) * LANES] = p1
                to_planes(g, cn, store_q)
        else:
            mixer = j - N_Q_CHUNKS
            k_ref[:, mixer * LANES:(mixer + 1) * LANES] = cn.astype(BF16)
            if mixer == MIXER_C:
                def store_k(refs, plane, part):
                    refs[1][plane] = part.astype(BF16)
                to_planes(2, cn, store_k)
    vals = y[:, MIX_WIDTH + KV_WIDTH:]
    v_ref[...] = vals.astype(BF16)

    def store_v(refs, plane, part):
        refs[2][plane] = part.astype(BF16)
    to_planes(3, vals[:, MIXER_C * LANES:(MIXER_C + 1) * LANES], store_v)


def _qkv_proj(x2, nw, w, gain, cos, sin, batch, seq, tm):
    n = x2.shape[0]
    tiles_per_seq = seq // tm
    out_specs = [
        pl.BlockSpec((tm, N_MIXERS * Q_SLOTS), lambda i: (i, 0)),
        pl.BlockSpec((tm, KV_WIDTH), lambda i: (i, 0)),
        pl.BlockSpec((tm, KV_WIDTH), lambda i: (i, 0)),
    ]
    out_shape = [
        jax.ShapeDtypeStruct((n, N_MIXERS * Q_SLOTS), BF16),
        jax.ShapeDtypeStruct((n, KV_WIDTH), BF16),
        jax.ShapeDtypeStruct((n, KV_WIDTH), BF16),
    ]
    for dil in PLANE_DILS:
        for width in (Q_SLOTS, MIXER_KV, MIXER_KV):
            out_specs.append(pl.BlockSpec((None, dil, tm // dil, width),
                                          lambda i: (i // tiles_per_seq, 0, i % tiles_per_seq, 0)))
            out_shape.append(jax.ShapeDtypeStruct((batch, dil, seq // dil, width), BF16))
    return pl.pallas_call(
        functools.partial(_qkv_kernel, tm=tm),
        grid=(n // tm,),
        in_specs=[
            pl.BlockSpec((tm, D_MODEL), lambda i: (i, 0)),
            _resident((1, D_MODEL)),
            _resident((D_MODEL, IN_WIDTH)),
            _resident((1, MIX_WIDTH + KV_WIDTH)),
            pl.BlockSpec((tm, LANES), lambda i: (i % tiles_per_seq, 0)),
            pl.BlockSpec((tm, LANES), lambda i: (i % tiles_per_seq, 0)),
        ],
        out_specs=out_specs,
        out_shape=out_shape,
        scratch_shapes=[pltpu.VMEM((4, tm, LANES), F32)],
        compiler_params=_cparams(("parallel",)),
        name="qkv_proj",
    )(x2, nw, w, gain, cos, sin)


def _stack_heads(qs):
    return jnp.concatenate([qs[:, h * LANES:(h + 1) * LANES] for h in range(Q_HEADS)], axis=0)


def _unstack_heads(o, rows, lo):
    c0 = jnp.where(lo, o[0:rows], o[2 * rows:3 * rows])
    c1 = jnp.where(lo, o[rows:2 * rows], o[3 * rows:4 * rows])
    return c0, c1


LOCAL_TOKENS_PER_STEP = 1024


def _local_kernel(*refs, dil, qb, kw, tq, length, radius, nbr, has_sink, want_lse, unroll):
    it = iter(refs)
    q_ref, k_ref, v_ref, tab_ref = next(it), next(it), next(it), next(it)
    sink_ref = next(it) if has_sink else None
    o_ref = next(it)
    lse_ref = next(it) if want_lse else None
    i = pl.program_id(1)
    nsub = tq // qb
    nblk = length // qb
    lo = lax.broadcasted_iota(jnp.int32, (1, LANES), 1) < HEAD_DIM

    def sub_block(f):
        plane, sb = (0, f) if dil == 1 else (f // nsub, f % nsub)
        blk = i * nsub + sb
        if nbr:
            rows = length // GRID_W
            r0 = jnp.clip(blk - NA_ROWS // 2, 0, rows - NA_ROWS)
            kstart = r0 * GRID_W
            var = blk - r0
        else:
            kstart = jnp.clip(blk * qb - radius, 0, length - kw)
            var = jnp.where(blk == 0, 0, jnp.where(blk == nblk - 1, 2, 1))
        kstart = pl.multiple_of(kstart, qb)
        qoff = pl.multiple_of(sb * qb, qb)
        qst = _stack_heads(q_ref[plane, pl.ds(qoff, qb), :])
        kb = k_ref[plane, pl.ds(kstart, kw), :]
        vb = v_ref[plane, pl.ds(kstart, kw), :]
        s = lax.dot_general(qst, kb, (((1,), (1,)), ((), ())), preferred_element_type=F32)
        s = s + tab_ref[var]
        n = Q_HEADS * qb
        m = jnp.broadcast_to(jnp.max(s, axis=-1, keepdims=True), (n, LANES))
        if has_sink:
            sk = sink_ref[...]
            m = jnp.maximum(m, sk)
        chunks = [jnp.exp(s[:, c * LANES:(c + 1) * LANES] - m) for c in range(kw // LANES)]
        if kw % LANES:
            chunks.append(jnp.exp(s[:, kw - kw % LANES:] - m[:, :kw % LANES]))
        p = jnp.concatenate(chunks, axis=-1)
        den = jnp.broadcast_to(jnp.sum(p, axis=-1, keepdims=True), (n, LANES))
        if has_sink:
            den = den + jnp.exp(sk - m)
        acc = jnp.dot(p.astype(BF16), vb, preferred_element_type=F32)
        out_rows = pl.ds(qoff, qb) if dil == 1 else pl.ds(qoff * dil + plane, qb, stride=dil)
        c0, c1 = _unstack_heads(acc / den, qb, lo)
        o_ref[0, out_rows, :] = c0
        o_ref[1, out_rows, :] = c1
        if want_lse:
            l0, l1 = _unstack_heads(m + jnp.log(den), qb, lo)
            lse_ref[0, out_rows, :] = l0
            lse_ref[1, out_rows, :] = l1

    def group(gi, carry):
        for u in range(unroll):
            sub_block(gi * unroll + u)
        return carry

    lax.fori_loop(0, dil * nsub // unroll, group, 0)


def _local_attention(q, k, v, tab, sink_tab, *, batch, seq, dil, col, qb, kw, radius, name, unroll,
                     nbr=False, want_lse=False):
    length = seq // dil
    tq = min(length, LOCAL_TOKENS_PER_STEP // dil)
    has_sink = sink_tab is not None
    kv_spec = pl.BlockSpec((None, dil, length, MIXER_KV), lambda b, i: (b, 0, 0, col))
    in_specs = [pl.BlockSpec((None, dil, tq, Q_SLOTS), lambda b, i: (b, 0, i, col)), kv_spec, kv_spec,
                _resident(tab.shape)]
    args = [q, k, v, tab]
    if has_sink:
        in_specs.append(_resident(sink_tab.shape))
        args.append(sink_tab)
    o_spec = pl.BlockSpec((None, MIXER_Q // LANES, tq * dil, LANES), lambda b, i: (b, 0, i, 0))
    o_shape = jax.ShapeDtypeStruct((batch, MIXER_Q // LANES, seq, LANES), F32)
    kern = functools.partial(_local_kernel, dil=dil, qb=qb, kw=kw, tq=tq, length=length, radius=radius,
                             nbr=nbr, has_sink=has_sink, want_lse=want_lse,
                             unroll=min(unroll, dil * (tq // qb)))
    outs = pl.pallas_call(
        kern,
        grid=(batch, length // tq),
        in_specs=in_specs,
        out_specs=[o_spec, o_spec] if want_lse else [o_spec],
        out_shape=[o_shape, o_shape] if want_lse else [o_shape],
        compiler_params=_cparams(("parallel", "arbitrary")),
        name=name,
    )(*args)
    return list(outs)


DENSE_CHAINS = 4


def _dense_kernel(q_ref, k_ref, v_ref, o_ref, *, tq, tk, seq):
    hq = tq // DENSE_CHAINS
    rows = Q_HEADS * hq
    qsts = [_stack_heads(q_ref[c * hq:(c + 1) * hq, :]) for c in range(DENSE_CHAINS)]

    def body(j, carry):
        koff = pl.multiple_of(j * tk, tk)
        kb = k_ref[pl.ds(koff, tk), :]
        vb = v_ref[pl.ds(koff, tk), :]
        out = []
        for c in range(DENSE_CHAINS):
            m, l, acc = carry[c]
            s = lax.dot_general(qsts[c], kb, (((1,), (1,)), ((), ())), preferred_element_type=F32)
            m_new = jnp.maximum(m, jnp.max(s, axis=-1, keepdims=True))
            alpha = jnp.exp2(m - m_new)
            p = jnp.exp2(s - m_new)
            l = alpha * l + jnp.sum(p, axis=-1, keepdims=True)
            acc = alpha * acc + jnp.dot(p.astype(BF16), vb, preferred_element_type=F32)
            out.append((m_new, l, acc))
        return tuple(out)

    init = tuple((jnp.full((rows, 1), NEG_INF, F32), jnp.zeros((rows, 1), F32), jnp.zeros((rows, LANES), F32))
                 for _ in range(DENSE_CHAINS))
    final = lax.fori_loop(0, seq // tk, body, init)
    lo = lax.broadcasted_iota(jnp.int32, (1, LANES), 1) < HEAD_DIM
    for c, (_, l, acc) in enumerate(final):
        c0, c1 = _unstack_heads(acc / l, hq, lo)
        o_ref[c * hq:(c + 1) * hq, 0:LANES] = c0
        o_ref[c * hq:(c + 1) * hq, LANES:2 * LANES] = c1


def _dense_attention(q, k, v, *, batch, seq, mixer, tq, tk):
    q3 = q.reshape(batch, seq, N_MIXERS * Q_SLOTS)
    k3 = k.reshape(batch, seq, KV_WIDTH)
    v3 = v.reshape(batch, seq, KV_WIDTH)
    out = pl.pallas_call(
        functools.partial(_dense_kernel, tq=tq, tk=tk, seq=seq),
        grid=(batch, seq // tq),
        in_specs=[
            pl.BlockSpec((None, tq, Q_SLOTS), lambda b, i: (b, i, mixer)),
            pl.BlockSpec((None, seq, MIXER_KV), lambda b, i: (b, 0, mixer)),
            pl.BlockSpec((None, seq, MIXER_KV), lambda b, i: (b, 0, mixer)),
        ],
        out_specs=pl.BlockSpec((None, tq, MIXER_Q), lambda b, i: (b, i, 0)),
        out_shape=jax.ShapeDtypeStruct((batch, seq, MIXER_Q), F32),
        compiler_params=_cparams(("parallel", "arbitrary")),
        name="dense_attn",
    )(q3, k3, v3)
    return out.reshape(batch * seq, MIXER_Q)


def _outproj_kernel(x_ref, oa_ref, ob_ref, oc1_ref, lc1_ref, oc2_ref, lc2_ref, oc3_ref, lc3_ref,
                    od_ref, gain_ref, w_ref, y_ref):
    def halves(ref):
        return jnp.concatenate([ref[0], ref[1]], axis=-1)

    l1, l2, l3 = halves(lc1_ref), halves(lc2_ref), halves(lc3_ref)
    mx = jnp.maximum(jnp.maximum(l1, l2), l3)
    e1, e2, e3 = jnp.exp(l1 - mx), jnp.exp(l2 - mx), jnp.exp(l3 - mx)
    oc = (e1 * halves(oc1_ref) + e2 * halves(oc2_ref) + e3 * halves(oc3_ref)) / (e1 + e2 + e3)
    acc = x_ref[...]
    for mixer, o in enumerate((halves(oa_ref), halves(ob_ref), oc, od_ref[...])):
        cols = slice(mixer * MIXER_Q, (mixer + 1) * MIXER_Q)
        ms = jnp.mean(o * o, axis=-1, keepdims=True)
        on = (o * lax.rsqrt(ms + EPS) * gain_ref[:, cols]).astype(BF16)
        acc = acc + jnp.dot(on, w_ref[cols, :], preferred_element_type=F32)
    y_ref[...] = acc


def _out_proj(x2, local_parts, o_dense, gain, w, seq, tm):
    n = x2.shape[0]
    tiles_per_seq = seq // tm
    row_spec = pl.BlockSpec((tm, D_MODEL), lambda i: (i, 0))
    local_spec = pl.BlockSpec((None, MIXER_Q // LANES, tm, LANES),
                              lambda i: (i // tiles_per_seq, 0, i % tiles_per_seq, 0))
    dense_spec = pl.BlockSpec((tm, MIXER_Q), lambda i: (i, 0))
    return pl.pallas_call(
        _outproj_kernel,
        grid=(n // tm,),
        in_specs=[row_spec] + [local_spec] * len(local_parts) + [dense_spec, _resident((1, MIX_WIDTH)),
                                                                 _resident((MIX_WIDTH, D_MODEL))],
        out_specs=row_spec,
        out_shape=jax.ShapeDtypeStruct((n, D_MODEL), F32),
        compiler_params=_cparams(("parallel",)),
        name="out_proj",
    )(x2, *local_parts, o_dense, gain, w)


def _ffn_kernel(x_ref, xp_ref, xn_ref, nw_ref, wg_ref, wv_ref, cw_ref, cb_ref, wd_ref, y_ref,
                h_ref, acc_ref, *, tm, tiles_per_seq):
    i = pl.program_id(0)
    first = (i % tiles_per_seq) == 0
    last = (i % tiles_per_seq) == tiles_per_seq - 1
    x = x_ref[...]
    xp = jnp.where(first, 0.0, xp_ref[...])
    xn = jnp.where(last, 0.0, xn_ref[...])
    xe = jnp.concatenate([xp, x, xn], axis=0)
    ms = jnp.mean(xe * xe, axis=-1, keepdims=True)
    h_ref[...] = (xe * lax.rsqrt(ms + EPS) * nw_ref[...]).astype(BF16)
    ext = tm + 2 * HALO

    for f in range(N_FF_CHUNKS):
        h = h_ref[...]
        g = jnp.dot(h, wg_ref[f], preferred_element_type=F32)
        u = jnp.dot(h, wv_ref[f], preferred_element_type=F32)
        cw = cw_ref[f]
        g_prev = pltpu.roll(g, 1, 0)
        g_next = pltpu.roll(g, ext - 1, 0)
        gc = cw[0:1] * g_prev + cw[1:2] * g + cw[2:3] * g_next + cb_ref[f]
        gc = gc[HALO:HALO + tm]
        act = 0.5 * gc * (1.0 + lax.erf(gc * (2.0 ** -0.5)))
        yv = (act * u[HALO:HALO + tm]).astype(BF16)
        d = jnp.dot(yv, wd_ref[f], preferred_element_type=F32)
        if f == 0:
            acc_ref[...] = x + d
        elif f < N_FF_CHUNKS - 1:
            acc_ref[...] += d
        else:
            y_ref[...] = acc_ref[...] + d


def _ffn(x2, nw, wg, wv, cw, cb, wd, seq, tm):
    n = x2.shape[0]
    tiles_per_seq = seq // tm
    blocks_per_tile = tm // HALO
    n_halo_blocks = n // HALO
    return pl.pallas_call(
        functools.partial(_ffn_kernel, tm=tm, tiles_per_seq=tiles_per_seq),
        grid=(n // tm,),
        in_specs=[
            pl.BlockSpec((tm, D_MODEL), lambda i: (i, 0)),
            pl.BlockSpec((HALO, D_MODEL), lambda i: (jnp.maximum(i * blocks_per_tile - 1, 0), 0)),
            pl.BlockSpec((HALO, D_MODEL),
                         lambda i: (jnp.minimum((i + 1) * blocks_per_tile, n_halo_blocks - 1), 0)),
            _resident((1, D_MODEL)),
            _resident((N_FF_CHUNKS, D_MODEL, FF_CHUNK)),
            _resident((N_FF_CHUNKS, D_MODEL, FF_CHUNK)),
            _resident((N_FF_CHUNKS, 3, FF_CHUNK)),
            _resident((N_FF_CHUNKS, 1, FF_CHUNK)),
            _resident((N_FF_CHUNKS, FF_CHUNK, D_MODEL)),
        ],
        out_specs=pl.BlockSpec((tm, D_MODEL), lambda i: (i, 0)),
        out_shape=jax.ShapeDtypeStruct((n, D_MODEL), F32),
        scratch_shapes=[pltpu.VMEM((tm + 2 * HALO, D_MODEL), BF16), pltpu.VMEM((tm, D_MODEL), F32)],
        compiler_params=_cparams(("parallel",)),
        name="conv_glu_ffn",
    )(x2, x2, x2, nw, wg, wv, cw, cb, wd)


def _alibi_slopes():
    n = 2 * Q_HEADS
    s = 2.0 ** (-8.0 * np.arange(1, n + 1, dtype=np.float64) / n)
    return s[0::2], s[1::2]


def _band_table(radius, qb, kw, slopes, step):
    qi = np.arange(qb)[:, None]
    ki = np.arange(kw)[None, :]
    out = np.empty((3, Q_HEADS, qb, kw), np.float32)
    for var, off in enumerate((0, radius, 2 * radius)):
        dist = np.abs(qi + off - ki)
        for h in range(Q_HEADS):
            out[var, h] = np.where(dist <= radius, -slopes[h] * step * dist, NEG_INF)
    return jnp.asarray(out.reshape(3, Q_HEADS * qb, kw))


def _nbr_table(rpb):
    by_row = jnp.stack([rpb[:, NA_ROWS - 1 - var:2 * NA_ROWS - 1 - var, :] for var in range(NA_ROWS)], axis=1)
    qc = np.arange(GRID_W)[:, None]
    c = np.arange(GRID_W)[None, :]
    cstart = np.clip(qc - NA_COLS // 2, 0, GRID_W - NA_COLS)
    valid = (c >= cstart) & (c < cstart + NA_COLS)
    dc = c - qc + (NA_COLS - 1)
    pick = ((dc[None] == np.arange(2 * NA_COLS - 1)[:, None, None]) & valid[None]).astype(np.float32)
    t = jnp.einsum("hvjd,dqc->vhqjc", by_row.astype(F32), jnp.asarray(pick), precision=lax.Precision.HIGHEST)
    t = jnp.where(valid[None, None, :, None, :], t, NEG_INF)
    return t.reshape(NA_ROWS, Q_HEADS * GRID_W, NA_ROWS * GRID_W)


def _rope_tables(seq):
    rot_row = HEAD_DIM // 2
    rot_col = HEAD_DIM - rot_row
    t = jnp.arange(seq)
    row = (t // GRID_W).astype(F32)
    col = (t % GRID_W).astype(F32)
    f_row = ROPE_THETA ** (-jnp.arange(0, rot_row, 2, dtype=F32) / rot_row)
    f_col = ROPE_THETA ** (-jnp.arange(0, rot_col, 2, dtype=F32) / rot_col)
    ang = jnp.concatenate([row[:, None] * f_row[None, :], col[:, None] * f_col[None, :]], axis=-1)
    cos, sin = jnp.cos(ang), jnp.sin(ang)
    cos_t = jnp.tile(jnp.concatenate([cos, cos], axis=-1), (1, LANES // HEAD_DIM))
    sin_t = jnp.tile(jnp.concatenate([-sin, sin], axis=-1), (1, LANES // HEAD_DIM))
    return cos_t, sin_t


def _even_odd(a):
    return jnp.concatenate([a[..., 0::2], a[..., 1::2]], axis=-1)


def _rotary_last_mixer(a, axis):
    head, tail = lax.slice_in_dim(a, 0, N_MIXERS - 1, axis=axis), lax.slice_in_dim(a, N_MIXERS - 1, N_MIXERS, axis=axis)
    return jnp.concatenate([head, _even_odd(tail)], axis=axis)


def _layer_params(norm1_w, w_in, q_norm_w, k_norm_w, sink_a, rpb_b, out_norm_w, w_out,
                  norm2_w, w_gate, w_val, conv_w, conv_b, w_down):
    groups = Q_HEADS // KV_HEADS
    wq = w_in[:, :MIX_WIDTH].reshape(D_MODEL, N_MIXERS, KV_HEADS, groups, HEAD_DIM)
    wq = _rotary_last_mixer(jnp.swapaxes(wq, 2, 3), 1).reshape(D_MODEL, MIX_WIDTH)
    wk = w_in[:, MIX_WIDTH:MIX_WIDTH + KV_WIDTH].reshape(D_MODEL, N_MIXERS, KV_HEADS, HEAD_DIM)
    wk = _rotary_last_mixer(wk, 1).reshape(D_MODEL, KV_WIDTH)
    w_in_p = jnp.concatenate([wq, wk, w_in[:, MIX_WIDTH + KV_WIDTH:]], axis=1).astype(BF16)
    q_scale = np.full((N_MIXERS, 1), HEAD_DIM ** -0.5, np.float32)
    q_scale[MIXER_D] *= LOG2E
    qg = _rotary_last_mixer(q_norm_w, 0) * q_scale
    kg = _rotary_last_mixer(k_norm_w, 0)
    gain = jnp.concatenate([jnp.broadcast_to(qg[:, None, :], (N_MIXERS, Q_HEADS, HEAD_DIM)).reshape(-1),
                            jnp.broadcast_to(kg[:, None, :], (N_MIXERS, KV_HEADS, HEAD_DIM)).reshape(-1)])
    out_gain = jnp.swapaxes(out_norm_w.reshape(N_MIXERS, KV_HEADS, groups, HEAD_DIM), 1, 2)
    w_out_p = jnp.swapaxes(w_out.reshape(N_MIXERS, KV_HEADS, groups, HEAD_DIM, D_MODEL), 1, 2)
    return dict(
        norm1_w=norm1_w.reshape(1, D_MODEL),
        w_in=w_in_p,
        qk_gain=gain.reshape(1, MIX_WIDTH + KV_WIDTH),
        sink=sink_a,
        rpb=rpb_b,
        out_gain=out_gain.reshape(1, MIX_WIDTH),
        w_out=w_out_p.reshape(MIX_WIDTH, D_MODEL).astype(BF16),
        norm2_w=norm2_w.reshape(1, D_MODEL),
        w_gate=jnp.transpose(w_gate.reshape(D_MODEL, N_FF_CHUNKS, FF_CHUNK), (1, 0, 2)).astype(BF16),
        w_val=jnp.transpose(w_val.reshape(D_MODEL, N_FF_CHUNKS, FF_CHUNK), (1, 0, 2)).astype(BF16),
        conv_w=jnp.transpose(conv_w.reshape(3, N_FF_CHUNKS, FF_CHUNK), (1, 0, 2)),
        conv_b=conv_b.reshape(N_FF_CHUNKS, 1, FF_CHUNK),
        w_down=w_down.reshape(N_FF_CHUNKS, FF_CHUNK, D_MODEL).astype(BF16),
    )


BAND_A = dict(qb=128, kw=3 * 128, radius=WIN_A, unroll=4)
BAND_C = dict(qb=64, kw=3 * 64, radius=64, unroll=16)
NBR_B = dict(qb=GRID_W, kw=NA_ROWS * GRID_W, radius=0, nbr=True, unroll=8)
ROW_TILE = 512
DENSE_TQ = 512
DENSE_TK = 2048


def _layer(x2, p, tabs, batch, seq):
    q, k, v, *planes = _qkv_proj(x2, p["norm1_w"], p["w_in"], p["qk_gain"], tabs["cos"], tabs["sin"],
                                 batch, seq, ROW_TILE)
    q4 = q.reshape(batch, 1, seq, N_MIXERS * Q_SLOTS)
    k4 = k.reshape(batch, 1, seq, KV_WIDTH)
    v4 = v.reshape(batch, 1, seq, KV_WIDTH)
    sink_tab = jnp.broadcast_to(jnp.repeat(p["sink"], BAND_A["qb"])[:, None], (Q_HEADS * BAND_A["qb"], LANES))
    (o_a,) = _local_attention(q4, k4, v4, tabs["band_a"], sink_tab, batch=batch, seq=seq, dil=1, col=0,
                              name="attn_a", **BAND_A)
    (o_b,) = _local_attention(q4, k4, v4, _nbr_table(p["rpb"]), None, batch=batch, seq=seq, dil=1, col=1,
                              name="attn_b", **NBR_B)
    c_parts = _local_attention(q4, k4, v4, tabs["band_c"][0], None, batch=batch, seq=seq, dil=1, col=MIXER_C,
                               want_lse=True, name="attn_c1", **BAND_C)
    for n, dil in enumerate(PLANE_DILS):
        qd, kd, vd = planes[3 * n:3 * n + 3]
        c_parts += _local_attention(qd, kd, vd, tabs["band_c"][n + 1], None, batch=batch, seq=seq, dil=dil,
                                    col=0, want_lse=True, name=f"attn_c{dil}", **BAND_C)
    o_d = _dense_attention(q, k, v, batch=batch, seq=seq, mixer=MIXER_D, tq=DENSE_TQ, tk=min(seq, DENSE_TK))
    x2 = _out_proj(x2, [o_a, o_b] + c_parts, o_d, p["out_gain"], p["w_out"], seq, ROW_TILE)
    return _ffn(x2, p["norm2_w"], p["w_gate"], p["w_val"], p["conv_w"], p["conv_b"], p["w_down"], seq, ROW_TILE)


def kernel(x_prompt, x_sample, norm1_w, w_in, q_norm_w, k_norm_w, sink_a, rpb_b, out_norm_w, w_out,
           norm2_w, w_gate, w_val, conv_w, conv_b, w_down):
    depth = norm1_w.shape[0]
    slopes_a, slopes_c = _alibi_slopes()
    band_a = _band_table(BAND_A["radius"], BAND_A["qb"], BAND_A["kw"], slopes_a, 1)
    band_c = [_band_table(BAND_C["radius"], BAND_C["qb"], BAND_C["kw"], slopes_c, dil) for _, dil in DILATIONS]
    layers = [_layer_params(norm1_w[l], w_in[l], q_norm_w[l], k_norm_w[l], sink_a[l], rpb_b[l],
                            out_norm_w[l], w_out[l], norm2_w[l], w_gate[l], w_val[l], conv_w[l],
                            conv_b[l], w_down[l]) for l in range(depth)]
    outs = []
    for x in (x_prompt, x_sample):
        batch, seq, _ = x.shape
        cos, sin = _rope_tables(seq)
        tabs = dict(cos=cos, sin=sin, band_a=band_a, band_c=band_c)
        x2 = x.reshape(batch * seq, D_MODEL)
        for p in layers:
            x2 = _layer(x2, p, tabs, batch, seq)
        outs.append(x2.reshape(batch, seq, D_MODEL))
    return tuple(outs)
```

```python
import functools
import math

import numpy as np
import jax
import jax.numpy as jnp
from jax import lax
from jax.experimental import pallas as pl
from jax.experimental.pallas import tpu as pltpu

D_MODEL = 1024
N_MIXERS = 4
Q_HEADS = 4
KV_HEADS = 2
HEAD_DIM = 64
MIX_WIDTH = N_MIXERS * Q_HEADS * HEAD_DIM
KV_WIDTH = N_MIXERS * KV_HEADS * HEAD_DIM
IN_WIDTH = MIX_WIDTH + 2 * KV_WIDTH
MIXER_Q = Q_HEADS * HEAD_DIM
MIXER_KV = KV_HEADS * HEAD_DIM
D_FF = 2816
GRID_W = 64
WIN_A = 128
NA_ROWS = 8
NA_COLS = 16
DILATIONS = ((128, 1), (512, 4), (2048, 16))
ROPE_THETA = 10000.0
EPS = 1e-6
NEG_INF = -1e30
LOG2E = math.log2(math.e)

LANES = 128
Q_SLOTS = Q_HEADS * LANES
FF_CHUNK = 256
N_FF_CHUNKS = D_FF // FF_CHUNK
HALO = 8
VMEM_LIMIT = 56 * 1024 * 1024

MIXER_C = 2
MIXER_D = 3
PLANE_DILS = tuple(d for _, d in DILATIONS if d > 1)

F32 = jnp.float32
BF16 = jnp.bfloat16


def _cparams(sem):
    return pltpu.CompilerParams(dimension_semantics=sem, vmem_limit_bytes=VMEM_LIMIT)


def _resident(shape):
    nd = len(shape)
    return pl.BlockSpec(shape, lambda *_: (0,) * nd, pipeline_mode=pl.Buffered(1))


N_Q_CHUNKS = MIX_WIDTH // LANES
N_QK_CHUNKS = (MIX_WIDTH + KV_WIDTH) // LANES
ROPE_CHUNKS = (2 * MIXER_D, 2 * MIXER_D + 1, N_Q_CHUNKS + MIXER_D)


def _qkv_kernel(x_ref, nw_ref, w_ref, gain_ref, cos_ref, sin_ref, q_ref, k_ref, v_ref, *rest, tm):
    plane_refs, perm_ref = rest[:-1], rest[-1]
    planes = {dil: plane_refs[3 * n:3 * n + 3] for n, dil in enumerate(PLANE_DILS)}
    x = x_ref[...]
    ms = jnp.mean(x * x, axis=-1, keepdims=True)
    h = (x * lax.rsqrt(ms + EPS) * nw_ref[...]).astype(BF16)
    y = jnp.dot(h, w_ref[...], preferred_element_type=F32)
    lane = lax.broadcasted_iota(jnp.int32, (1, LANES), 1)
    lo = lane < HEAD_DIM
    first_half = (lane % HEAD_DIM) < HEAD_DIM // 2
    cos = cos_ref[...]
    sin = sin_ref[...]

    def q_slots(c):
        return jnp.where(lo, c, 0.0).astype(BF16), jnp.where(lo, 0.0, c).astype(BF16)

    def to_planes(slot, c, store):
        perm_ref[slot] = c
        for dil in PLANE_DILS:
            for j in range(dil):
                store(planes[dil], j, perm_ref[slot, pl.ds(j, tm // dil, stride=dil), :])

    for j in range(N_QK_CHUNKS):
        c = y[:, j * LANES:(j + 1) * LANES]
        c2 = c * c
        tot = jnp.sum(c2, axis=-1, keepdims=True)
        s_lo = jnp.sum(jnp.where(lo, c2, 0.0), axis=-1, keepdims=True)
        ssq = jnp.where(lo, s_lo, tot - s_lo)
        cn = c * lax.rsqrt(ssq + HEAD_DIM * EPS) * gain_ref[:, j * LANES:(j + 1) * LANES]
        if j in ROPE_CHUNKS:
            partner = jnp.where(first_half,
                                pltpu.roll(cn, LANES - HEAD_DIM // 2, 1),
                                pltpu.roll(cn, HEAD_DIM // 2, 1))
            cn = cn * cos + partner * sin
        if j < N_Q_CHUNKS:
            mixer, g = divmod(j, 2)
            base = mixer * Q_SLOTS
            s0, s1 = q_slots(cn)
            q_ref[:, base + g * LANES:base + (g + 1) * LANES] = s0
            q_ref[:, base + (2 + g) * LANES:base + (3 + g) * LANES] = s1
            if mixer == MIXER_C:
                def store_q(refs, plane, part, g=g):
                    p0, p1 = q_slots(part)
                    refs[0][plane, :, g * LANES:(g + 1) * LANES] = p0
                    refs[0][plane, :, (2 + g) * LANES:(3 + g) * LANES] = p1
                to_planes(g, cn, store_q)
        else:
            mixer = j - N_Q_CHUNKS
            k_ref[:, mixer * LANES:(mixer + 1) * LANES] = cn.astype(BF16)
            if mixer == MIXER_C:
                def store_k(refs, plane, part):
                    refs[1][plane] = part.astype(BF16)
                to_planes(2, cn, store_k)
    vals = y[:, MIX_WIDTH + KV_WIDTH:]
    v_ref[...] = vals.astype(BF16)

    def store_v(refs, plane, part):
        refs[2][plane] = part.astype(BF16)
    to_planes(3, vals[:, MIXER_C * LANES:(MIXER_C + 1) * LANES], store_v)


def _qkv_proj(x2, nw, w, gain, cos, sin, batch, seq, tm):
    n = x2.shape[0]
    tiles_per_seq = seq // tm
    out_specs = [
        pl.BlockSpec((tm, N_MIXERS * Q_SLOTS), lambda i: (i, 0)),
        pl.BlockSpec((tm, KV_WIDTH), lambda i: (i, 0)),
        pl.BlockSpec((tm, KV_WIDTH), lambda i: (i, 0)),
    ]
    out_shape = [
        jax.ShapeDtypeStruct((n, N_MIXERS * Q_SLOTS), BF16),
        jax.ShapeDtypeStruct((n, KV_WIDTH), BF16),
        jax.ShapeDtypeStruct((n, KV_WIDTH), BF16),
    ]
    for dil in PLANE_DILS:
        for width in (Q_SLOTS, MIXER_KV, MIXER_KV):
            out_specs.append(pl.BlockSpec((None, dil, tm // dil, width),
                                          lambda i: (i // tiles_per_seq, 0, i % tiles_per_seq, 0)))
            out_shape.append(jax.ShapeDtypeStruct((batch, dil, seq // dil, width), BF16))
    return pl.pallas_call(
        functools.partial(_qkv_kernel, tm=tm),
        grid=(n // tm,),
        in_specs=[
            pl.BlockSpec((tm, D_MODEL), lambda i: (i, 0)),
            _resident((1, D_MODEL)),
            _resident((D_MODEL, IN_WIDTH)),
            _resident((1, MIX_WIDTH + KV_WIDTH)),
            pl.BlockSpec((tm, LANES), lambda i: (i % tiles_per_seq, 0)),
            pl.BlockSpec((tm, LANES), lambda i: (i % tiles_per_seq, 0)),
        ],
        out_specs=out_specs,
        out_shape=out_shape,
        scratch_shapes=[pltpu.VMEM((4, tm, LANES), F32)],
        compiler_params=_cparams(("parallel",)),
        name="qkv_proj",
    )(x2, nw, w, gain, cos, sin)


def _stack_heads(qs):
    return jnp.concatenate([qs[:, h * LANES:(h + 1) * LANES] for h in range(Q_HEADS)], axis=0)


def _unstack_heads(o, rows, lo):
    c0 = jnp.where(lo, o[0:rows], o[2 * rows:3 * rows])
    c1 = jnp.where(lo, o[rows:2 * rows], o[3 * rows:4 * rows])
    return c0, c1


LOCAL_TOKENS_PER_STEP = 1024


def _local_kernel(*refs, dil, qb, kw, tq, length, radius, nbr, has_sink, want_lse, unroll):
    it = iter(refs)
    q_ref, k_ref, v_ref, tab_ref = next(it), next(it), next(it), next(it)
    sink_ref = next(it) if has_sink else None
    o_ref = next(it)
    lse_ref = next(it) if want_lse else None
    i = pl.program_id(1)
    nsub = tq // qb
    nblk = length // qb
    lo = lax.broadcasted_iota(jnp.int32, (1, LANES), 1) < HEAD_DIM

    def sub_block(f):
        plane, sb = (0, f) if dil == 1 else (f // nsub, f % nsub)
        blk = i * nsub + sb
        if nbr:
            rows = length // GRID_W
            r0 = jnp.clip(blk - NA_ROWS // 2, 0, rows - NA_ROWS)
            kstart = r0 * GRID_W
            var = blk - r0
        else:
            kstart = jnp.clip(blk * qb - radius, 0, length - kw)
            var = jnp.where(blk == 0, 0, jnp.where(blk == nblk - 1, 2, 1))
        kstart = pl.multiple_of(kstart, qb)
        qoff = pl.multiple_of(sb * qb, qb)
        qst = _stack_heads(q_ref[plane, pl.ds(qoff, qb), :])
        kb = k_ref[plane, pl.ds(kstart, kw), :]
        vb = v_ref[plane, pl.ds(kstart, kw), :]
        s = lax.dot_general(qst, kb, (((1,), (1,)), ((), ())), preferred_element_type=F32)
        s = s + tab_ref[var]
        n = Q_HEADS * qb
        m = jnp.broadcast_to(jnp.max(s, axis=-1, keepdims=True), (n, LANES))
        if has_sink:
            sk = sink_ref[...]
            m = jnp.maximum(m, sk)
        chunks = [jnp.exp2(s[:, c * LANES:(c + 1) * LANES] - m) for c in range(kw // LANES)]
        if kw % LANES:
            chunks.append(jnp.exp2(s[:, kw - kw % LANES:] - m[:, :kw % LANES]))
        p = jnp.concatenate(chunks, axis=-1)
        den = jnp.broadcast_to(jnp.sum(p, axis=-1, keepdims=True), (n, LANES))
        if has_sink:
            den = den + jnp.exp2(sk - m)
        acc = jnp.dot(p.astype(BF16), vb, preferred_element_type=F32)
        out_rows = pl.ds(qoff, qb) if dil == 1 else pl.ds(qoff * dil + plane, qb, stride=dil)
        c0, c1 = _unstack_heads(acc / den, qb, lo)
        o_ref[0, out_rows, :] = c0
        o_ref[1, out_rows, :] = c1
        if want_lse:
            l0, l1 = _unstack_heads(m + jnp.log2(den), qb, lo)
            lse_ref[0, out_rows, :] = l0
            lse_ref[1, out_rows, :] = l1

    def group(gi, carry):
        for u in range(unroll):
            sub_block(gi * unroll + u)
        return carry

    lax.fori_loop(0, dil * nsub // unroll, group, 0)


def _local_attention(q, k, v, tab, sink_tab, *, batch, seq, dil, col, qb, kw, radius, name, unroll,
                     nbr=False, want_lse=False):
    length = seq // dil
    tq = min(length, LOCAL_TOKENS_PER_STEP // dil)
    has_sink = sink_tab is not None
    kv_spec = pl.BlockSpec((None, dil, length, MIXER_KV), lambda b, i: (b, 0, 0, col))
    in_specs = [pl.BlockSpec((None, dil, tq, Q_SLOTS), lambda b, i: (b, 0, i, col)), kv_spec, kv_spec,
                _resident(tab.shape)]
    args = [q, k, v, tab]
    if has_sink:
        in_specs.append(_resident(sink_tab.shape))
        args.append(sink_tab)
    o_spec = pl.BlockSpec((None, MIXER_Q // LANES, tq * dil, LANES), lambda b, i: (b, 0, i, 0))
    o_shape = jax.ShapeDtypeStruct((batch, MIXER_Q // LANES, seq, LANES), F32)
    kern = functools.partial(_local_kernel, dil=dil, qb=qb, kw=kw, tq=tq, length=length, radius=radius,
                             nbr=nbr, has_sink=has_sink, want_lse=want_lse,
                             unroll=min(unroll, dil * (tq // qb)))
    outs = pl.pallas_call(
        kern,
        grid=(batch, length // tq),
        in_specs=in_specs,
        out_specs=[o_spec, o_spec] if want_lse else [o_spec],
        out_shape=[o_shape, o_shape] if want_lse else [o_shape],
        compiler_params=_cparams(("parallel", "arbitrary")),
        name=name,
    )(*args)
    return list(outs)


DENSE_CHAINS = 4


def _dense_kernel(q_ref, k_ref, v_ref, o_ref, *, tq, tk, seq):
    hq = tq // DENSE_CHAINS
    rows = Q_HEADS * hq
    qsts = [_stack_heads(q_ref[c * hq:(c + 1) * hq, :]) for c in range(DENSE_CHAINS)]

    def body(j, carry):
        koff = pl.multiple_of(j * tk, tk)
        kb = k_ref[pl.ds(koff, tk), :]
        vb = v_ref[pl.ds(koff, tk), :]
        out = []
        for c in range(DENSE_CHAINS):
            m, l, acc = carry[c]
            s = lax.dot_general(qsts[c], kb, (((1,), (1,)), ((), ())), preferred_element_type=F32)
            m_new = jnp.maximum(m, jnp.max(s, axis=-1, keepdims=True))
            alpha = jnp.exp2(m - m_new)
            p = jnp.exp2(s - m_new)
            l = alpha * l + jnp.sum(p, axis=-1, keepdims=True)
            acc = alpha * acc + jnp.dot(p.astype(BF16), vb, preferred_element_type=F32)
            out.append((m_new, l, acc))
        return tuple(out)

    init = tuple((jnp.full((rows, 1), NEG_INF, F32), jnp.zeros((rows, 1), F32), jnp.zeros((rows, LANES), F32))
                 for _ in range(DENSE_CHAINS))
    final = lax.fori_loop(0, seq // tk, body, init, unroll=2)
    lo = lax.broadcasted_iota(jnp.int32, (1, LANES), 1) < HEAD_DIM
    for c, (_, l, acc) in enumerate(final):
        c0, c1 = _unstack_heads(acc / l, hq, lo)
        o_ref[c * hq:(c + 1) * hq, 0:LANES] = c0
        o_ref[c * hq:(c + 1) * hq, LANES:2 * LANES] = c1


def _dense_attention(q, k, v, *, batch, seq, mixer, tq, tk):
    q3 = q.reshape(batch, seq, N_MIXERS * Q_SLOTS)
    k3 = k.reshape(batch, seq, KV_WIDTH)
    v3 = v.reshape(batch, seq, KV_WIDTH)
    out = pl.pallas_call(
        functools.partial(_dense_kernel, tq=tq, tk=tk, seq=seq),
        grid=(batch, seq // tq),
        in_specs=[
            pl.BlockSpec((None, tq, Q_SLOTS), lambda b, i: (b, i, mixer)),
            pl.BlockSpec((None, seq, MIXER_KV), lambda b, i: (b, 0, mixer)),
            pl.BlockSpec((None, seq, MIXER_KV), lambda b, i: (b, 0, mixer)),
        ],
        out_specs=pl.BlockSpec((None, tq, MIXER_Q), lambda b, i: (b, i, 0)),
        out_shape=jax.ShapeDtypeStruct((batch, seq, MIXER_Q), F32),
        compiler_params=_cparams(("parallel", "arbitrary")),
        name="dense_attn",
    )(q3, k3, v3)
    return out.reshape(batch * seq, MIXER_Q)


def _outproj_kernel(x_ref, oa_ref, ob_ref, oc1_ref, lc1_ref, oc2_ref, lc2_ref, oc3_ref, lc3_ref,
                    od_ref, gain_ref, w_ref, y_ref):
    def halves(ref):
        return jnp.concatenate([ref[0], ref[1]], axis=-1)

    l1, l2, l3 = halves(lc1_ref), halves(lc2_ref), halves(lc3_ref)
    mx = jnp.maximum(jnp.maximum(l1, l2), l3)
    e1, e2, e3 = jnp.exp2(l1 - mx), jnp.exp2(l2 - mx), jnp.exp2(l3 - mx)
    oc = (e1 * halves(oc1_ref) + e2 * halves(oc2_ref) + e3 * halves(oc3_ref)) / (e1 + e2 + e3)
    normed = []
    for mixer, o in enumerate((halves(oa_ref), halves(ob_ref), oc, od_ref[...])):
        ms = jnp.mean(o * o, axis=-1, keepdims=True)
        gain = gain_ref[:, mixer * MIXER_Q:(mixer + 1) * MIXER_Q]
        normed.append((o * lax.rsqrt(ms + EPS) * gain).astype(BF16))
    y_ref[...] = x_ref[...] + jnp.dot(jnp.concatenate(normed, axis=-1), w_ref[...],
                                      preferred_element_type=F32)


def _out_proj(x2, local_parts, o_dense, gain, w, seq, tm):
    n = x2.shape[0]
    tiles_per_seq = seq // tm
    row_spec = pl.BlockSpec((tm, D_MODEL), lambda i: (i, 0))
    local_spec = pl.BlockSpec((None, MIXER_Q // LANES, tm, LANES),
                              lambda i: (i // tiles_per_seq, 0, i % tiles_per_seq, 0))
    dense_spec = pl.BlockSpec((tm, MIXER_Q), lambda i: (i, 0))
    return pl.pallas_call(
        _outproj_kernel,
        grid=(n // tm,),
        in_specs=[row_spec] + [local_spec] * len(local_parts) + [dense_spec, _resident((1, MIX_WIDTH)),
                                                                 _resident((MIX_WIDTH, D_MODEL))],
        out_specs=row_spec,
        out_shape=jax.ShapeDtypeStruct((n, D_MODEL), F32),
        compiler_params=_cparams(("parallel",)),
        name="out_proj",
    )(x2, *local_parts, o_dense, gain, w)


def _ffn_kernel(x_ref, xp_ref, xn_ref, nw_ref, wg_ref, wv_ref, cw_ref, cb_ref, wd_ref, y_ref,
                h_ref, act_ref, *, tm, tiles_per_seq):
    i = pl.program_id(0)
    first = (i % tiles_per_seq) == 0
    last = (i % tiles_per_seq) == tiles_per_seq - 1
    x = x_ref[...]
    xp = jnp.where(first, 0.0, xp_ref[...])
    xn = jnp.where(last, 0.0, xn_ref[...])
    xe = jnp.concatenate([xp, x, xn], axis=0)
    ms = jnp.mean(xe * xe, axis=-1, keepdims=True)
    h_ref[...] = (xe * lax.rsqrt(ms + EPS) * nw_ref[...]).astype(BF16)
    ext = tm + 2 * HALO

    for f in range(N_FF_CHUNKS):
        h = h_ref[...]
        g = jnp.dot(h, wg_ref[f], preferred_element_type=F32)
        u = jnp.dot(h, wv_ref[f], preferred_element_type=F32)
        cw = cw_ref[f]
        g_prev = pltpu.roll(g, 1, 0)
        g_next = pltpu.roll(g, ext - 1, 0)
        gc = cw[0:1] * g_prev + cw[1:2] * g + cw[2:3] * g_next + cb_ref[f]
        gc = gc[HALO:HALO + tm]
        act = 0.5 * gc * (1.0 + lax.erf(gc * (2.0 ** -0.5)))
        act_ref[:, f * FF_CHUNK:(f + 1) * FF_CHUNK] = (act * u[HALO:HALO + tm]).astype(BF16)
    y_ref[...] = x + jnp.dot(act_ref[...], wd_ref[...], preferred_element_type=F32)


def _ffn(x2, nw, wg, wv, cw, cb, wd, seq, tm):
    n = x2.shape[0]
    tiles_per_seq = seq // tm
    blocks_per_tile = tm // HALO
    n_halo_blocks = n // HALO
    return pl.pallas_call(
        functools.partial(_ffn_kernel, tm=tm, tiles_per_seq=tiles_per_seq),
        grid=(n // tm,),
        in_specs=[
            pl.BlockSpec((tm, D_MODEL), lambda i: (i, 0)),
            pl.BlockSpec((HALO, D_MODEL), lambda i: (jnp.maximum(i * blocks_per_tile - 1, 0), 0)),
            pl.BlockSpec((HALO, D_MODEL),
                         lambda i: (jnp.minimum((i + 1) * blocks_per_tile, n_halo_blocks - 1), 0)),
            _resident((1, D_MODEL)),
            _resident((N_FF_CHUNKS, D_MODEL, FF_CHUNK)),
            _resident((N_FF_CHUNKS, D_MODEL, FF_CHUNK)),
            _resident((N_FF_CHUNKS, 3, FF_CHUNK)),
            _resident((N_FF_CHUNKS, 1, FF_CHUNK)),
            _resident((D_FF, D_MODEL)),
        ],
        out_specs=pl.BlockSpec((tm, D_MODEL), lambda i: (i, 0)),
        out_shape=jax.ShapeDtypeStruct((n, D_MODEL), F32),
        scratch_shapes=[pltpu.VMEM((tm + 2 * HALO, D_MODEL), BF16), pltpu.VMEM((tm, D_FF), BF16)],
        compiler_params=_cparams(("parallel",)),
        name="conv_glu_ffn",
    )(x2, x2, x2, nw, wg, wv, cw, cb, wd)


def _alibi_slopes():
    n = 2 * Q_HEADS
    s = 2.0 ** (-8.0 * np.arange(1, n + 1, dtype=np.float64) / n)
    return s[0::2], s[1::2]


def _band_table(radius, qb, kw, slopes, step):
    qi = np.arange(qb)[:, None]
    ki = np.arange(kw)[None, :]
    out = np.empty((3, Q_HEADS, qb, kw), np.float32)
    for var, off in enumerate((0, radius, 2 * radius)):
        dist = np.abs(qi + off - ki)
        for h in range(Q_HEADS):
            out[var, h] = np.where(dist <= radius, -slopes[h] * step * dist * LOG2E, NEG_INF)
    return jnp.asarray(out.reshape(3, Q_HEADS * qb, kw))


def _nbr_table(rpb):
    by_row = jnp.stack([rpb[:, NA_ROWS - 1 - var:2 * NA_ROWS - 1 - var, :] for var in range(NA_ROWS)], axis=1)
    qc = np.arange(GRID_W)[:, None]
    c = np.arange(GRID_W)[None, :]
    cstart = np.clip(qc - NA_COLS // 2, 0, GRID_W - NA_COLS)
    valid = (c >= cstart) & (c < cstart + NA_COLS)
    dc = c - qc + (NA_COLS - 1)
    pick = ((dc[None] == np.arange(2 * NA_COLS - 1)[:, None, None]) & valid[None]).astype(np.float32)
    t = jnp.einsum("hvjd,dqc->vhqjc", by_row.astype(F32), jnp.asarray(pick), precision=lax.Precision.HIGHEST)
    t = jnp.where(valid[None, None, :, None, :], t * LOG2E, NEG_INF)
    return t.reshape(NA_ROWS, Q_HEADS * GRID_W, NA_ROWS * GRID_W)


def _rope_tables(seq):
    rot_row = HEAD_DIM // 2
    rot_col = HEAD_DIM - rot_row
    t = jnp.arange(seq)
    row = (t // GRID_W).astype(F32)
    col = (t % GRID_W).astype(F32)
    f_row = ROPE_THETA ** (-jnp.arange(0, rot_row, 2, dtype=F32) / rot_row)
    f_col = ROPE_THETA ** (-jnp.arange(0, rot_col, 2, dtype=F32) / rot_col)
    ang = jnp.concatenate([row[:, None] * f_row[None, :], col[:, None] * f_col[None, :]], axis=-1)
    cos, sin = jnp.cos(ang), jnp.sin(ang)
    cos_t = jnp.tile(jnp.concatenate([cos, cos], axis=-1), (1, LANES // HEAD_DIM))
    sin_t = jnp.tile(jnp.concatenate([-sin, sin], axis=-1), (1, LANES // HEAD_DIM))
    return cos_t, sin_t


def _even_odd(a):
    return jnp.concatenate([a[..., 0::2], a[..., 1::2]], axis=-1)


def _rotary_last_mixer(a, axis):
    head, tail = lax.slice_in_dim(a, 0, N_MIXERS - 1, axis=axis), lax.slice_in_dim(a, N_MIXERS - 1, N_MIXERS, axis=axis)
    return jnp.concatenate([head, _even_odd(tail)], axis=axis)


def _layer_params(norm1_w, w_in, q_norm_w, k_norm_w, sink_a, rpb_b, out_norm_w, w_out,
                  norm2_w, w_gate, w_val, conv_w, conv_b, w_down):
    groups = Q_HEADS // KV_HEADS
    wq = w_in[:, :MIX_WIDTH].reshape(D_MODEL, N_MIXERS, KV_HEADS, groups, HEAD_DIM)
    wq = _rotary_last_mixer(jnp.swapaxes(wq, 2, 3), 1).reshape(D_MODEL, MIX_WIDTH)
    wk = w_in[:, MIX_WIDTH:MIX_WIDTH + KV_WIDTH].reshape(D_MODEL, N_MIXERS, KV_HEADS, HEAD_DIM)
    wk = _rotary_last_mixer(wk, 1).reshape(D_MODEL, KV_WIDTH)
    w_in_p = jnp.concatenate([wq, wk, w_in[:, MIX_WIDTH + KV_WIDTH:]], axis=1).astype(BF16)
    qg = _rotary_last_mixer(q_norm_w, 0) * LOG2E
    kg = _rotary_last_mixer(k_norm_w, 0) * HEAD_DIM ** 0.5
    gain = jnp.concatenate([jnp.broadcast_to(qg[:, None, :], (N_MIXERS, Q_HEADS, HEAD_DIM)).reshape(-1),
                            jnp.broadcast_to(kg[:, None, :], (N_MIXERS, KV_HEADS, HEAD_DIM)).reshape(-1)])
    out_gain = jnp.swapaxes(out_norm_w.reshape(N_MIXERS, KV_HEADS, groups, HEAD_DIM), 1, 2)
    w_out_p = jnp.swapaxes(w_out.reshape(N_MIXERS, KV_HEADS, groups, HEAD_DIM, D_MODEL), 1, 2)
    return dict(
        norm1_w=norm1_w.reshape(1, D_MODEL),
        w_in=w_in_p,
        qk_gain=gain.reshape(1, MIX_WIDTH + KV_WIDTH),
        sink=sink_a,
        rpb=rpb_b,
        out_gain=out_gain.reshape(1, MIX_WIDTH),
        w_out=w_out_p.reshape(MIX_WIDTH, D_MODEL).astype(BF16),
        norm2_w=norm2_w.reshape(1, D_MODEL),
        w_gate=jnp.transpose(w_gate.reshape(D_MODEL, N_FF_CHUNKS, FF_CHUNK), (1, 0, 2)).astype(BF16),
        w_val=jnp.transpose(w_val.reshape(D_MODEL, N_FF_CHUNKS, FF_CHUNK), (1, 0, 2)).astype(BF16),
        conv_w=jnp.transpose(conv_w.reshape(3, N_FF_CHUNKS, FF_CHUNK), (1, 0, 2)),
        conv_b=conv_b.reshape(N_FF_CHUNKS, 1, FF_CHUNK),
        w_down=w_down.astype(BF16),
    )


BAND_A = dict(qb=128, kw=3 * 128, radius=WIN_A, unroll=4)
BAND_C = dict(qb=64, kw=3 * 64, radius=64, unroll=16)
NBR_B = dict(qb=GRID_W, kw=NA_ROWS * GRID_W, radius=0, nbr=True, unroll=8)
ROW_TILE = 512
DENSE_TQ = 512
DENSE_TK = 2048


def _layer(x2, p, tabs, batch, seq):
    q, k, v, *planes = _qkv_proj(x2, p["norm1_w"], p["w_in"], p["qk_gain"], tabs["cos"], tabs["sin"],
                                 batch, seq, ROW_TILE)
    q4 = q.reshape(batch, 1, seq, N_MIXERS * Q_SLOTS)
    k4 = k.reshape(batch, 1, seq, KV_WIDTH)
    v4 = v.reshape(batch, 1, seq, KV_WIDTH)
    sink_tab = jnp.broadcast_to(jnp.repeat(p["sink"] * LOG2E, BAND_A["qb"])[:, None],
                                (Q_HEADS * BAND_A["qb"], LANES))
    (o_a,) = _local_attention(q4, k4, v4, tabs["band_a"], sink_tab, batch=batch, seq=seq, dil=1, col=0,
                              name="attn_a", **BAND_A)
    (o_b,) = _local_attention(q4, k4, v4, _nbr_table(p["rpb"]), None, batch=batch, seq=seq, dil=1, col=1,
                              name="attn_b", **NBR_B)
    c_parts = _local_attention(q4, k4, v4, tabs["band_c"][0], None, batch=batch, seq=seq, dil=1, col=MIXER_C,
                               want_lse=True, name="attn_c1", **BAND_C)
    for n, dil in enumerate(PLANE_DILS):
        qd, kd, vd = planes[3 * n:3 * n + 3]
        c_parts += _local_attention(qd, kd, vd, tabs["band_c"][n + 1], None, batch=batch, seq=seq, dil=dil,
                                    col=0, want_lse=True, name=f"attn_c{dil}", **BAND_C)
    o_d = _dense_attention(q, k, v, batch=batch, seq=seq, mixer=MIXER_D, tq=DENSE_TQ, tk=min(seq, DENSE_TK))
    x2 = _out_proj(x2, [o_a, o_b] + c_parts, o_d, p["out_gain"], p["w_out"], seq, ROW_TILE)
    return _ffn(x2, p["norm2_w"], p["w_gate"], p["w_val"], p["conv_w"], p["conv_b"], p["w_down"], seq, ROW_TILE)


def kernel(x_prompt, x_sample, norm1_w, w_in, q_norm_w, k_norm_w, sink_a, rpb_b, out_norm_w, w_out,
           norm2_w, w_gate, w_val, conv_w, conv_b, w_down):
    depth = norm1_w.shape[0]
    slopes_a, slopes_c = _alibi_slopes()
    band_a = _band_table(BAND_A["radius"], BAND_A["qb"], BAND_A["kw"], slopes_a, 1)
    band_c = [_band_table(BAND_C["radius"], BAND_C["qb"], BAND_C["kw"], slopes_c, dil) for _, dil in DILATIONS]
    layers = [_layer_params(norm1_w[l], w_in[l], q_norm_w[l], k_norm_w[l], sink_a[l], rpb_b[l],
                            out_norm_w[l], w_out[l], norm2_w[l], w_gate[l], w_val[l], conv_w[l],
                            conv_b[l], w_down[l]) for l in range(depth)]
    outs = []
    for x in (x_prompt, x_sample):
        batch, seq, _ = x.shape
        cos, sin = _rope_tables(seq)
        tabs = dict(cos=cos, sin=sin, band_a=band_a, band_c=band_c)
        x2 = x.reshape(batch * seq, D_MODEL)
        for p in layers:
            x2 = _layer(x2, p, tabs, batch, seq)
        outs.append(x2.reshape(batch, seq, D_MODEL))
    return tuple(outs)
```

```python
import functools
import math

import numpy as np
import jax
import jax.numpy as jnp
from jax import lax
from jax.experimental import pallas as pl
from jax.experimental.pallas import tpu as pltpu

D_MODEL = 1024
N_MIXERS = 4
Q_HEADS = 4
KV_HEADS = 2
HEAD_DIM = 64
MIX_WIDTH = N_MIXERS * Q_HEADS * HEAD_DIM
KV_WIDTH = N_MIXERS * KV_HEADS * HEAD_DIM
IN_WIDTH = MIX_WIDTH + 2 * KV_WIDTH
MIXER_Q = Q_HEADS * HEAD_DIM
MIXER_KV = KV_HEADS * HEAD_DIM
D_FF = 2816
GRID_W = 64
WIN_A = 128
NA_ROWS = 8
NA_COLS = 16
DILATIONS = ((128, 1), (512, 4), (2048, 16))
ROPE_THETA = 10000.0
EPS = 1e-6
NEG_INF = -1e30
LOG2E = math.log2(math.e)

LANES = 128
Q_SLOTS = Q_HEADS * LANES
FF_CHUNK = 256
N_FF_CHUNKS = D_FF // FF_CHUNK
HALO = 8
VMEM_LIMIT = 56 * 1024 * 1024

MIXER_C = 2
MIXER_D = 3
PLANE_DILS = tuple(d for _, d in DILATIONS if d > 1)

F32 = jnp.float32
BF16 = jnp.bfloat16


def _cparams(sem):
    return pltpu.CompilerParams(dimension_semantics=sem, vmem_limit_bytes=VMEM_LIMIT)


def _resident(shape):
    nd = len(shape)
    return pl.BlockSpec(shape, lambda *_: (0,) * nd, pipeline_mode=pl.Buffered(1))


N_Q_CHUNKS = MIX_WIDTH // LANES
N_QK_CHUNKS = (MIX_WIDTH + KV_WIDTH) // LANES
ROPE_CHUNKS = (2 * MIXER_D, 2 * MIXER_D + 1, N_Q_CHUNKS + MIXER_D)


def _qkv_kernel(x_ref, nw_ref, w_ref, gain_ref, cos_ref, sin_ref, q_ref, k_ref, v_ref, *rest, tm):
    plane_refs, perm_ref = rest[:-1], rest[-1]
    planes = {dil: plane_refs[3 * n:3 * n + 3] for n, dil in enumerate(PLANE_DILS)}
    x = x_ref[...]
    ms = jnp.mean(x * x, axis=-1, keepdims=True)
    h = (x * lax.rsqrt(ms + EPS) * nw_ref[...]).astype(BF16)
    y = jnp.dot(h, w_ref[...], preferred_element_type=F32)
    lane = lax.broadcasted_iota(jnp.int32, (1, LANES), 1)
    lo = lane < HEAD_DIM
    first_half = (lane % HEAD_DIM) < HEAD_DIM // 2
    cos = cos_ref[...]
    sin = sin_ref[...]

    def q_slots(c):
        return jnp.where(lo, c, 0.0).astype(BF16), jnp.where(lo, 0.0, c).astype(BF16)

    def to_planes(slot, c, store):
        perm_ref[slot] = c
        for dil in PLANE_DILS:
            for j in range(dil):
                store(planes[dil], j, perm_ref[slot, pl.ds(j, tm // dil, stride=dil), :])

    for j in range(N_QK_CHUNKS):
        c = y[:, j * LANES:(j + 1) * LANES]
        c2 = c * c
        tot = jnp.sum(c2, axis=-1, keepdims=True)
        s_lo = jnp.sum(jnp.where(lo, c2, 0.0), axis=-1, keepdims=True)
        ssq = jnp.where(lo, s_lo, tot - s_lo)
        cn = c * lax.rsqrt(ssq + HEAD_DIM * EPS) * gain_ref[:, j * LANES:(j + 1) * LANES]
        if j in ROPE_CHUNKS:
            partner = jnp.where(first_half,
                                pltpu.roll(cn, LANES - HEAD_DIM // 2, 1),
                                pltpu.roll(cn, HEAD_DIM // 2, 1))
            cn = cn * cos + partner * sin
        if j < N_Q_CHUNKS:
            mixer, g = divmod(j, 2)
            base = mixer * Q_SLOTS
            s0, s1 = q_slots(cn)
            q_ref[:, base + g * LANES:base + (g + 1) * LANES] = s0
            q_ref[:, base + (2 + g) * LANES:base + (3 + g) * LANES] = s1
            if mixer == MIXER_C:
                def store_q(refs, plane, part, g=g):
                    p0, p1 = q_slots(part)
                    refs[0][plane, :, g * LANES:(g + 1) * LANES] = p0
                    refs[0][plane, :, (2 + g) * LANES:(3 + g) * LANES] = p1
                to_planes(g, cn, store_q)
        else:
            mixer = j - N_Q_CHUNKS
            k_ref[:, mixer * LANES:(mixer + 1) * LANES] = cn.astype(BF16)
            if mixer == MIXER_C:
                def store_k(refs, plane, part):
                    refs[1][plane] = part.astype(BF16)
                to_planes(2, cn, store_k)
    vals = y[:, MIX_WIDTH + KV_WIDTH:]
    v_ref[...] = vals.astype(BF16)

    def store_v(refs, plane, part):
        refs[2][plane] = part.astype(BF16)
    to_planes(3, vals[:, MIXER_C * LANES:(MIXER_C + 1) * LANES], store_v)


def _qkv_proj(x2, nw, w, gain, cos, sin, batch, seq, tm):
    n = x2.shape[0]
    tiles_per_seq = seq // tm
    out_specs = [
        pl.BlockSpec((tm, N_MIXERS * Q_SLOTS), lambda i: (i, 0)),
        pl.BlockSpec((tm, KV_WIDTH), lambda i: (i, 0)),
        pl.BlockSpec((tm, KV_WIDTH), lambda i: (i, 0)),
    ]
    out_shape = [
        jax.ShapeDtypeStruct((n, N_MIXERS * Q_SLOTS), BF16),
        jax.ShapeDtypeStruct((n, KV_WIDTH), BF16),
        jax.ShapeDtypeStruct((n, KV_WIDTH), BF16),
    ]
    for dil in PLANE_DILS:
        for width in (Q_SLOTS, MIXER_KV, MIXER_KV):
            out_specs.append(pl.BlockSpec((None, dil, tm // dil, width),
                                          lambda i: (i // tiles_per_seq, 0, i % tiles_per_seq, 0)))
            out_shape.append(jax.ShapeDtypeStruct((batch, dil, seq // dil, width), BF16))
    return pl.pallas_call(
        functools.partial(_qkv_kernel, tm=tm),
        grid=(n // tm,),
        in_specs=[
            pl.BlockSpec((tm, D_MODEL), lambda i: (i, 0)),
            _resident((1, D_MODEL)),
            _resident((D_MODEL, IN_WIDTH)),
            _resident((1, MIX_WIDTH + KV_WIDTH)),
            pl.BlockSpec((tm, LANES), lambda i: (i % tiles_per_seq, 0)),
            pl.BlockSpec((tm, LANES), lambda i: (i % tiles_per_seq, 0)),
        ],
        out_specs=out_specs,
        out_shape=out_shape,
        scratch_shapes=[pltpu.VMEM((4, tm, LANES), F32)],
        compiler_params=_cparams(("parallel",)),
        name="qkv_proj",
    )(x2, nw, w, gain, cos, sin)


def _stack_heads(qs):
    return jnp.concatenate([qs[:, h * LANES:(h + 1) * LANES] for h in range(Q_HEADS)], axis=0)


def _unstack_heads(o, rows, lo):
    c0 = jnp.where(lo, o[0:rows], o[2 * rows:3 * rows])
    c1 = jnp.where(lo, o[rows:2 * rows], o[3 * rows:4 * rows])
    return c0, c1


LOCAL_TOKENS_PER_STEP = 1024


def _local_kernel(*refs, dil, qb, kw, tq, length, radius, nbr, has_sink, want_lse, unroll, mxu_rowsum):
    it = iter(refs)
    q_ref, k_ref, v_ref, tab_ref = next(it), next(it), next(it), next(it)
    sink_ref = next(it) if has_sink else None
    o_ref = next(it)
    lse_ref = next(it) if want_lse else None
    i = pl.program_id(1)
    nsub = tq // qb
    nblk = length // qb
    lo = lax.broadcasted_iota(jnp.int32, (1, LANES), 1) < HEAD_DIM

    def sub_block(f):
        plane, sb = (0, f) if dil == 1 else (f // nsub, f % nsub)
        blk = i * nsub + sb
        if nbr:
            rows = length // GRID_W
            r0 = jnp.clip(blk - NA_ROWS // 2, 0, rows - NA_ROWS)
            kstart = r0 * GRID_W
            var = blk - r0
        else:
            kstart = jnp.clip(blk * qb - radius, 0, length - kw)
            var = jnp.where(blk == 0, 0, jnp.where(blk == nblk - 1, 2, 1))
        kstart = pl.multiple_of(kstart, qb)
        qoff = pl.multiple_of(sb * qb, qb)
        qst = _stack_heads(q_ref[plane, pl.ds(qoff, qb), :])
        kb = k_ref[plane, pl.ds(kstart, kw), :]
        vb = v_ref[plane, pl.ds(kstart, kw), :]
        s = lax.dot_general(qst, kb, (((1,), (1,)), ((), ())), preferred_element_type=F32)
        s = s + tab_ref[var]
        n = Q_HEADS * qb
        m = jnp.broadcast_to(jnp.max(s, axis=-1, keepdims=True), (n, LANES))
        if has_sink:
            sk = sink_ref[...]
            m = jnp.maximum(m, sk)
        chunks = [jnp.exp2(s[:, c * LANES:(c + 1) * LANES] - m) for c in range(kw // LANES)]
        if kw % LANES:
            chunks.append(jnp.exp2(s[:, kw - kw % LANES:] - m[:, :kw % LANES]))
        p32 = jnp.concatenate(chunks, axis=-1)
        p = p32.astype(BF16)
        if mxu_rowsum:
            both = jnp.dot(p, jnp.concatenate([vb, jnp.ones((kw, LANES), BF16)], axis=-1),
                           preferred_element_type=F32)
            acc, den = both[:, :LANES], both[:, LANES:]
        else:
            den = jnp.broadcast_to(jnp.sum(p32, axis=-1, keepdims=True), (n, LANES))
            acc = jnp.dot(p, vb, preferred_element_type=F32)
        if has_sink:
            den = den + jnp.exp2(sk - m)
        out_rows = pl.ds(qoff, qb) if dil == 1 else pl.ds(qoff * dil + plane, qb, stride=dil)
        c0, c1 = _unstack_heads(acc / den, qb, lo)
        o_ref[0, out_rows, :] = c0
        o_ref[1, out_rows, :] = c1
        if want_lse:
            l0, l1 = _unstack_heads(m + jnp.log2(den), qb, lo)
            lse_ref[0, out_rows, :] = l0
            lse_ref[1, out_rows, :] = l1

    def group(gi, carry):
        for u in range(unroll):
            sub_block(gi * unroll + u)
        return carry

    lax.fori_loop(0, dil * nsub // unroll, group, 0)


def _local_attention(q, k, v, tab, sink_tab, *, batch, seq, dil, col, qb, kw, radius, name, unroll,
                     mxu_rowsum, nbr=False, want_lse=False):
    length = seq // dil
    tq = min(length, LOCAL_TOKENS_PER_STEP // dil)
    has_sink = sink_tab is not None
    kv_spec = pl.BlockSpec((None, dil, length, MIXER_KV), lambda b, i: (b, 0, 0, col))
    in_specs = [pl.BlockSpec((None, dil, tq, Q_SLOTS), lambda b, i: (b, 0, i, col)), kv_spec, kv_spec,
                _resident(tab.shape)]
    args = [q, k, v, tab]
    if has_sink:
        in_specs.append(_resident(sink_tab.shape))
        args.append(sink_tab)
    o_spec = pl.BlockSpec((None, MIXER_Q // LANES, tq * dil, LANES), lambda b, i: (b, 0, i, 0))
    o_shape = jax.ShapeDtypeStruct((batch, MIXER_Q // LANES, seq, LANES), F32)
    kern = functools.partial(_local_kernel, dil=dil, qb=qb, kw=kw, tq=tq, length=length, radius=radius,
                             nbr=nbr, has_sink=has_sink, want_lse=want_lse, mxu_rowsum=mxu_rowsum,
                             unroll=min(unroll, dil * (tq // qb)))
    outs = pl.pallas_call(
        kern,
        grid=(batch, length // tq),
        in_specs=in_specs,
        out_specs=[o_spec, o_spec] if want_lse else [o_spec],
        out_shape=[o_shape, o_shape] if want_lse else [o_shape],
        compiler_params=_cparams(("parallel", "arbitrary")),
        name=name,
    )(*args)
    return list(outs)


DENSE_CHAINS = 4


def _dense_kernel(q_ref, k_ref, v_ref, o_ref, *, tq, tk, seq):
    hq = tq // DENSE_CHAINS
    rows = Q_HEADS * hq
    qsts = [_stack_heads(q_ref[c * hq:(c + 1) * hq, :]) for c in range(DENSE_CHAINS)]

    def body(j, carry):
        koff = pl.multiple_of(j * tk, tk)
        kb = k_ref[pl.ds(koff, tk), :]
        vb = v_ref[pl.ds(koff, tk), :]
        out = []
        for c in range(DENSE_CHAINS):
            m, l, acc = carry[c]
            s = lax.dot_general(qsts[c], kb, (((1,), (1,)), ((), ())), preferred_element_type=F32)
            m_new = jnp.maximum(m, jnp.max(s, axis=-1, keepdims=True))
            alpha = jnp.exp2(m - m_new)
            p = jnp.exp2(s - m_new)
            l = alpha * l + jnp.sum(p, axis=-1, keepdims=True)
            acc = alpha * acc + jnp.dot(p.astype(BF16), vb, preferred_element_type=F32)
            out.append((m_new, l, acc))
        return tuple(out)

    init = tuple((jnp.full((rows, 1), NEG_INF, F32), jnp.zeros((rows, 1), F32), jnp.zeros((rows, LANES), F32))
                 for _ in range(DENSE_CHAINS))
    final = lax.fori_loop(0, seq // tk, body, init, unroll=2)
    lo = lax.broadcasted_iota(jnp.int32, (1, LANES), 1) < HEAD_DIM
    for c, (_, l, acc) in enumerate(final):
        c0, c1 = _unstack_heads(acc / l, hq, lo)
        o_ref[c * hq:(c + 1) * hq, 0:LANES] = c0
        o_ref[c * hq:(c + 1) * hq, LANES:2 * LANES] = c1


def _dense_attention(q, k, v, *, batch, seq, mixer, tq, tk):
    q3 = q.reshape(batch, seq, N_MIXERS * Q_SLOTS)
    k3 = k.reshape(batch, seq, KV_WIDTH)
    v3 = v.reshape(batch, seq, KV_WIDTH)
    out = pl.pallas_call(
        functools.partial(_dense_kernel, tq=tq, tk=tk, seq=seq),
        grid=(batch, seq // tq),
        in_specs=[
            pl.BlockSpec((None, tq, Q_SLOTS), lambda b, i: (b, i, mixer)),
            pl.BlockSpec((None, seq, MIXER_KV), lambda b, i: (b, 0, mixer)),
            pl.BlockSpec((None, seq, MIXER_KV), lambda b, i: (b, 0, mixer)),
        ],
        out_specs=pl.BlockSpec((None, tq, MIXER_Q), lambda b, i: (b, i, 0)),
        out_shape=jax.ShapeDtypeStruct((batch, seq, MIXER_Q), F32),
        compiler_params=_cparams(("parallel", "arbitrary")),
        name="dense_attn",
    )(q3, k3, v3)
    return out.reshape(batch * seq, MIXER_Q)


def _outproj_kernel(x_ref, oa_ref, ob_ref, oc1_ref, lc1_ref, oc2_ref, lc2_ref, oc3_ref, lc3_ref,
                    od_ref, gain_ref, w_ref, y_ref):
    def halves(ref):
        return jnp.concatenate([ref[0], ref[1]], axis=-1)

    l1, l2, l3 = halves(lc1_ref), halves(lc2_ref), halves(lc3_ref)
    mx = jnp.maximum(jnp.maximum(l1, l2), l3)
    e1, e2, e3 = jnp.exp2(l1 - mx), jnp.exp2(l2 - mx), jnp.exp2(l3 - mx)
    oc = (e1 * halves(oc1_ref) + e2 * halves(oc2_ref) + e3 * halves(oc3_ref)) / (e1 + e2 + e3)
    normed = []
    for mixer, o in enumerate((halves(oa_ref), halves(ob_ref), oc, od_ref[...])):
        ms = jnp.mean(o * o, axis=-1, keepdims=True)
        gain = gain_ref[:, mixer * MIXER_Q:(mixer + 1) * MIXER_Q]
        normed.append((o * lax.rsqrt(ms + EPS) * gain).astype(BF16))
    y_ref[...] = x_ref[...] + jnp.dot(jnp.concatenate(normed, axis=-1), w_ref[...],
                                      preferred_element_type=F32)


def _out_proj(x2, local_parts, o_dense, gain, w, seq, tm):
    n = x2.shape[0]
    tiles_per_seq = seq // tm
    row_spec = pl.BlockSpec((tm, D_MODEL), lambda i: (i, 0))
    local_spec = pl.BlockSpec((None, MIXER_Q // LANES, tm, LANES),
                              lambda i: (i // tiles_per_seq, 0, i % tiles_per_seq, 0))
    dense_spec = pl.BlockSpec((tm, MIXER_Q), lambda i: (i, 0))
    return pl.pallas_call(
        _outproj_kernel,
        grid=(n // tm,),
        in_specs=[row_spec] + [local_spec] * len(local_parts) + [dense_spec, _resident((1, MIX_WIDTH)),
                                                                 _resident((MIX_WIDTH, D_MODEL))],
        out_specs=row_spec,
        out_shape=jax.ShapeDtypeStruct((n, D_MODEL), F32),
        compiler_params=_cparams(("parallel",)),
        name="out_proj",
    )(x2, *local_parts, o_dense, gain, w)


def _ffn_kernel(x_ref, xp_ref, xn_ref, nw_ref, wg_ref, wv_ref, cw_ref, cb_ref, wd_ref, y_ref,
                h_ref, act_ref, *, tm, tiles_per_seq):
    i = pl.program_id(0)
    first = (i % tiles_per_seq) == 0
    last = (i % tiles_per_seq) == tiles_per_seq - 1
    x = x_ref[...]
    xp = jnp.where(first, 0.0, xp_ref[...])
    xn = jnp.where(last, 0.0, xn_ref[...])
    xe = jnp.concatenate([xp, x, xn], axis=0)
    ms = jnp.mean(xe * xe, axis=-1, keepdims=True)
    h_ref[...] = (xe * lax.rsqrt(ms + EPS) * nw_ref[...]).astype(BF16)
    ext = tm + 2 * HALO

    for f in range(N_FF_CHUNKS):
        h = h_ref[...]
        g = jnp.dot(h, wg_ref[f], preferred_element_type=F32)
        u = jnp.dot(h, wv_ref[f], preferred_element_type=F32)
        cw = cw_ref[f]
        g_prev = pltpu.roll(g, 1, 0)
        g_next = pltpu.roll(g, ext - 1, 0)
        gc = cw[0:1] * g_prev + cw[1:2] * g + cw[2:3] * g_next + cb_ref[f]
        gc = gc[HALO:HALO + tm]
        act = 0.5 * gc * (1.0 + lax.erf(gc * (2.0 ** -0.5)))
        act_ref[:, f * FF_CHUNK:(f + 1) * FF_CHUNK] = (act * u[HALO:HALO + tm]).astype(BF16)
    y_ref[...] = x + jnp.dot(act_ref[...], wd_ref[...], preferred_element_type=F32)


def _ffn(x2, nw, wg, wv, cw, cb, wd, seq, tm):
    n = x2.shape[0]
    tiles_per_seq = seq // tm
    blocks_per_tile = tm // HALO
    n_halo_blocks = n // HALO
    return pl.pallas_call(
        functools.partial(_ffn_kernel, tm=tm, tiles_per_seq=tiles_per_seq),
        grid=(n // tm,),
        in_specs=[
            pl.BlockSpec((tm, D_MODEL), lambda i: (i, 0)),
            pl.BlockSpec((HALO, D_MODEL), lambda i: (jnp.maximum(i * blocks_per_tile - 1, 0), 0)),
            pl.BlockSpec((HALO, D_MODEL),
                         lambda i: (jnp.minimum((i + 1) * blocks_per_tile, n_halo_blocks - 1), 0)),
            _resident((1, D_MODEL)),
            _resident((N_FF_CHUNKS, D_MODEL, FF_CHUNK)),
            _resident((N_FF_CHUNKS, D_MODEL, FF_CHUNK)),
            _resident((N_FF_CHUNKS, 3, FF_CHUNK)),
            _resident((N_FF_CHUNKS, 1, FF_CHUNK)),
            _resident((D_FF, D_MODEL)),
        ],
        out_specs=pl.BlockSpec((tm, D_MODEL), lambda i: (i, 0)),
        out_shape=jax.ShapeDtypeStruct((n, D_MODEL), F32),
        scratch_shapes=[pltpu.VMEM((tm + 2 * HALO, D_MODEL), BF16), pltpu.VMEM((tm, D_FF), BF16)],
        compiler_params=_cparams(("parallel",)),
        name="conv_glu_ffn",
    )(x2, x2, x2, nw, wg, wv, cw, cb, wd)


def _alibi_slopes():
    n = 2 * Q_HEADS
    s = 2.0 ** (-8.0 * np.arange(1, n + 1, dtype=np.float64) / n)
    return s[0::2], s[1::2]


def _band_table(radius, qb, kw, slopes, step):
    qi = np.arange(qb)[:, None]
    ki = np.arange(kw)[None, :]
    out = np.empty((3, Q_HEADS, qb, kw), np.float32)
    for var, off in enumerate((0, radius, 2 * radius)):
        dist = np.abs(qi + off - ki)
        for h in range(Q_HEADS):
            out[var, h] = np.where(dist <= radius, -slopes[h] * step * dist * LOG2E, NEG_INF)
    return jnp.asarray(out.reshape(3, Q_HEADS * qb, kw))


def _nbr_table(rpb):
    by_row = jnp.stack([rpb[:, NA_ROWS - 1 - var:2 * NA_ROWS - 1 - var, :] for var in range(NA_ROWS)], axis=1)
    qc = np.arange(GRID_W)[:, None]
    c = np.arange(GRID_W)[None, :]
    cstart = np.clip(qc - NA_COLS // 2, 0, GRID_W - NA_COLS)
    valid = (c >= cstart) & (c < cstart + NA_COLS)
    dc = c - qc + (NA_COLS - 1)
    pick = ((dc[None] == np.arange(2 * NA_COLS - 1)[:, None, None]) & valid[None]).astype(np.float32)
    t = jnp.einsum("hvjd,dqc->vhqjc", by_row.astype(F32), jnp.asarray(pick), precision=lax.Precision.HIGHEST)
    t = jnp.where(valid[None, None, :, None, :], t * LOG2E, NEG_INF)
    return t.reshape(NA_ROWS, Q_HEADS * GRID_W, NA_ROWS * GRID_W)


def _rope_tables(seq):
    rot_row = HEAD_DIM // 2
    rot_col = HEAD_DIM - rot_row
    t = jnp.arange(seq)
    row = (t // GRID_W).astype(F32)
    col = (t % GRID_W).astype(F32)
    f_row = ROPE_THETA ** (-jnp.arange(0, rot_row, 2, dtype=F32) / rot_row)
    f_col = ROPE_THETA ** (-jnp.arange(0, rot_col, 2, dtype=F32) / rot_col)
    ang = jnp.concatenate([row[:, None] * f_row[None, :], col[:, None] * f_col[None, :]], axis=-1)
    cos, sin = jnp.cos(ang), jnp.sin(ang)
    cos_t = jnp.tile(jnp.concatenate([cos, cos], axis=-1), (1, LANES // HEAD_DIM))
    sin_t = jnp.tile(jnp.concatenate([-sin, sin], axis=-1), (1, LANES // HEAD_DIM))
    return cos_t, sin_t


def _even_odd(a):
    return jnp.concatenate([a[..., 0::2], a[..., 1::2]], axis=-1)


def _rotary_last_mixer(a, axis):
    head, tail = lax.slice_in_dim(a, 0, N_MIXERS - 1, axis=axis), lax.slice_in_dim(a, N_MIXERS - 1, N_MIXERS, axis=axis)
    return jnp.concatenate([head, _even_odd(tail)], axis=axis)


def _layer_params(norm1_w, w_in, q_norm_w, k_norm_w, sink_a, rpb_b, out_norm_w, w_out,
                  norm2_w, w_gate, w_val, conv_w, conv_b, w_down):
    groups = Q_HEADS // KV_HEADS
    wq = w_in[:, :MIX_WIDTH].reshape(D_MODEL, N_MIXERS, KV_HEADS, groups, HEAD_DIM)
    wq = _rotary_last_mixer(jnp.swapaxes(wq, 2, 3), 1).reshape(D_MODEL, MIX_WIDTH)
    wk = w_in[:, MIX_WIDTH:MIX_WIDTH + KV_WIDTH].reshape(D_MODEL, N_MIXERS, KV_HEADS, HEAD_DIM)
    wk = _rotary_last_mixer(wk, 1).reshape(D_MODEL, KV_WIDTH)
    w_in_p = jnp.concatenate([wq, wk, w_in[:, MIX_WIDTH + KV_WIDTH:]], axis=1).astype(BF16)
    qg = _rotary_last_mixer(q_norm_w, 0) * LOG2E
    kg = _rotary_last_mixer(k_norm_w, 0) * HEAD_DIM ** 0.5
    gain = jnp.concatenate([jnp.broadcast_to(qg[:, None, :], (N_MIXERS, Q_HEADS, HEAD_DIM)).reshape(-1),
                            jnp.broadcast_to(kg[:, None, :], (N_MIXERS, KV_HEADS, HEAD_DIM)).reshape(-1)])
    out_gain = jnp.swapaxes(out_norm_w.reshape(N_MIXERS, KV_HEADS, groups, HEAD_DIM), 1, 2)
    w_out_p = jnp.swapaxes(w_out.reshape(N_MIXERS, KV_HEADS, groups, HEAD_DIM, D_MODEL), 1, 2)
    return dict(
        norm1_w=norm1_w.reshape(1, D_MODEL),
        w_in=w_in_p,
        qk_gain=gain.reshape(1, MIX_WIDTH + KV_WIDTH),
        sink=sink_a,
        rpb=rpb_b,
        out_gain=out_gain.reshape(1, MIX_WIDTH),
        w_out=w_out_p.reshape(MIX_WIDTH, D_MODEL).astype(BF16),
        norm2_w=norm2_w.reshape(1, D_MODEL),
        w_gate=jnp.transpose(w_gate.reshape(D_MODEL, N_FF_CHUNKS, FF_CHUNK), (1, 0, 2)).astype(BF16),
        w_val=jnp.transpose(w_val.reshape(D_MODEL, N_FF_CHUNKS, FF_CHUNK), (1, 0, 2)).astype(BF16),
        conv_w=jnp.transpose(conv_w.reshape(3, N_FF_CHUNKS, FF_CHUNK), (1, 0, 2)),
        conv_b=conv_b.reshape(N_FF_CHUNKS, 1, FF_CHUNK),
        w_down=w_down.astype(BF16),
    )


BAND_A = dict(qb=128, kw=3 * 128, radius=WIN_A, unroll=4, mxu_rowsum=False)
BAND_C = dict(qb=64, kw=3 * 64, radius=64, unroll=16, mxu_rowsum=True)
NBR_B = dict(qb=GRID_W, kw=NA_ROWS * GRID_W, radius=0, nbr=True, unroll=16, mxu_rowsum=True)
ROW_TILE = 512
DENSE_TQ = 512
DENSE_TK = 2048


def _layer(x2, p, tabs, batch, seq):
    q, k, v, *planes = _qkv_proj(x2, p["norm1_w"], p["w_in"], p["qk_gain"], tabs["cos"], tabs["sin"],
                                 batch, seq, ROW_TILE)
    q4 = q.reshape(batch, 1, seq, N_MIXERS * Q_SLOTS)
    k4 = k.reshape(batch, 1, seq, KV_WIDTH)
    v4 = v.reshape(batch, 1, seq, KV_WIDTH)
    sink_tab = jnp.broadcast_to(jnp.repeat(p["sink"] * LOG2E, BAND_A["qb"])[:, None],
                                (Q_HEADS * BAND_A["qb"], LANES))
    (o_a,) = _local_attention(q4, k4, v4, tabs["band_a"], sink_tab, batch=batch, seq=seq, dil=1, col=0,
                              name="attn_a", **BAND_A)
    (o_b,) = _local_attention(q4, k4, v4, _nbr_table(p["rpb"]), None, batch=batch, seq=seq, dil=1, col=1,
                              name="attn_b", **NBR_B)
    c_parts = _local_attention(q4, k4, v4, tabs["band_c"][0], None, batch=batch, seq=seq, dil=1, col=MIXER_C,
                               want_lse=True, name="attn_c1", **BAND_C)
    for n, dil in enumerate(PLANE_DILS):
        qd, kd, vd = planes[3 * n:3 * n + 3]
        c_parts += _local_attention(qd, kd, vd, tabs["band_c"][n + 1], None, batch=batch, seq=seq, dil=dil,
                                    col=0, want_lse=True, name=f"attn_c{dil}", **BAND_C)
    o_d = _dense_attention(q, k, v, batch=batch, seq=seq, mixer=MIXER_D, tq=DENSE_TQ, tk=min(seq, DENSE_TK))
    x2 = _out_proj(x2, [o_a, o_b] + c_parts, o_d, p["out_gain"], p["w_out"], seq, ROW_TILE)
    return _ffn(x2, p["norm2_w"], p["w_gate"], p["w_val"], p["conv_w"], p["conv_b"], p["w_down"], seq, ROW_TILE)


def kernel(x_prompt, x_sample, norm1_w, w_in, q_norm_w, k_norm_w, sink_a, rpb_b, out_norm_w, w_out,
           norm2_w, w_gate, w_val, conv_w, conv_b, w_down):
    depth = norm1_w.shape[0]
    slopes_a, slopes_c = _alibi_slopes()
    band_a = _band_table(BAND_A["radius"], BAND_A["qb"], BAND_A["kw"], slopes_a, 1)
    band_c = [_band_table(BAND_C["radius"], BAND_C["qb"], BAND_C["kw"], slopes_c, dil) for _, dil in DILATIONS]
    layers = [_layer_params(norm1_w[l], w_in[l], q_norm_w[l], k_norm_w[l], sink_a[l], rpb_b[l],
                            out_norm_w[l], w_out[l], norm2_w[l], w_gate[l], w_val[l], conv_w[l],
                            conv_b[l], w_down[l]) for l in range(depth)]
    outs = []
    for x in (x_prompt, x_sample):
        batch, seq, _ = x.shape
        cos, sin = _rope_tables(seq)
        tabs = dict(cos=cos, sin=sin, band_a=band_a, band_c=band_c)
        x2 = x.reshape(batch * seq, D_MODEL)
        for p in layers:
            x2 = _layer(x2, p, tabs, batch, seq)
        outs.append(x2.reshape(batch, seq, D_MODEL))
    return tuple(outs)
```

```python
import functools
import math

import numpy as np
import jax
import jax.numpy as jnp
from jax import lax
from jax.experimental import pallas as pl
from jax.experimental.pallas import tpu as pltpu

D_MODEL = 1024
N_MIXERS = 4
Q_HEADS = 4
KV_HEADS = 2
HEAD_DIM = 64
MIX_WIDTH = N_MIXERS * Q_HEADS * HEAD_DIM
KV_WIDTH = N_MIXERS * KV_HEADS * HEAD_DIM
IN_WIDTH = MIX_WIDTH + 2 * KV_WIDTH
MIXER_Q = Q_HEADS * HEAD_DIM
MIXER_KV = KV_HEADS * HEAD_DIM
D_FF = 2816
GRID_W = 64
WIN_A = 128
NA_ROWS = 8
NA_COLS = 16
DILATIONS = ((128, 1), (512, 4), (2048, 16))
ROPE_THETA = 10000.0
EPS = 1e-6
NEG_INF = -1e30
LOG2E = math.log2(math.e)

LANES = 128
Q_SLOTS = Q_HEADS * LANES
FF_CHUNK = 256
N_FF_CHUNKS = D_FF // FF_CHUNK
HALO = 8
VMEM_LIMIT = 56 * 1024 * 1024

MIXER_C = 2
MIXER_D = 3
PLANE_DILS = tuple(d for _, d in DILATIONS if d > 1)

F32 = jnp.float32
BF16 = jnp.bfloat16


def _cparams(sem):
    return pltpu.CompilerParams(dimension_semantics=sem, vmem_limit_bytes=VMEM_LIMIT)


def _resident(shape):
    nd = len(shape)
    return pl.BlockSpec(shape, lambda *_: (0,) * nd, pipeline_mode=pl.Buffered(1))


N_Q_CHUNKS = MIX_WIDTH // LANES
N_QK_CHUNKS = (MIX_WIDTH + KV_WIDTH) // LANES
ROPE_CHUNKS = (2 * MIXER_D, 2 * MIXER_D + 1, N_Q_CHUNKS + MIXER_D)


def _qkv_kernel(x_ref, nw_ref, w_ref, gain_ref, cos_ref, sin_ref, q_ref, k_ref, v_ref, *rest, tm):
    plane_refs, perm_ref = rest[:-1], rest[-1]
    planes = {dil: plane_refs[3 * n:3 * n + 3] for n, dil in enumerate(PLANE_DILS)}
    x = x_ref[...]
    ms = jnp.mean(x * x, axis=-1, keepdims=True)
    h = (x * lax.rsqrt(ms + EPS) * nw_ref[...]).astype(BF16)
    y = jnp.dot(h, w_ref[...], preferred_element_type=F32)
    lane = lax.broadcasted_iota(jnp.int32, (1, LANES), 1)
    lo = lane < HEAD_DIM
    first_half = (lane % HEAD_DIM) < HEAD_DIM // 2
    cos = cos_ref[...]
    sin = sin_ref[...]

    def q_slots(c):
        return jnp.where(lo, c, 0.0).astype(BF16), jnp.where(lo, 0.0, c).astype(BF16)

    def to_planes(slot, c, store):
        perm_ref[slot] = c
        for dil in PLANE_DILS:
            for j in range(dil):
                store(planes[dil], j, perm_ref[slot, pl.ds(j, tm // dil, stride=dil), :])

    for j in range(N_QK_CHUNKS):
        c = y[:, j * LANES:(j + 1) * LANES]
        c2 = c * c
        tot = jnp.sum(c2, axis=-1, keepdims=True)
        s_lo = jnp.sum(jnp.where(lo, c2, 0.0), axis=-1, keepdims=True)
        ssq = jnp.where(lo, s_lo, tot - s_lo)
        cn = c * lax.rsqrt(ssq + HEAD_DIM * EPS) * gain_ref[:, j * LANES:(j + 1) * LANES]
        if j in ROPE_CHUNKS:
            partner = jnp.where(first_half,
                                pltpu.roll(cn, LANES - HEAD_DIM // 2, 1),
                                pltpu.roll(cn, HEAD_DIM // 2, 1))
            cn = cn * cos + partner * sin
        if j < N_Q_CHUNKS:
            mixer, g = divmod(j, 2)
            base = mixer * Q_SLOTS
            s0, s1 = q_slots(cn)
            q_ref[:, base + g * LANES:base + (g + 1) * LANES] = s0
            q_ref[:, base + (2 + g) * LANES:base + (3 + g) * LANES] = s1
            if mixer == MIXER_C:
                def store_q(refs, plane, part, g=g):
                    p0, p1 = q_slots(part)
                    refs[0][plane, :, g * LANES:(g + 1) * LANES] = p0
                    refs[0][plane, :, (2 + g) * LANES:(3 + g) * LANES] = p1
                to_planes(g, cn, store_q)
        else:
            mixer = j - N_Q_CHUNKS
            k_ref[:, mixer * LANES:(mixer + 1) * LANES] = cn.astype(BF16)
            if mixer == MIXER_C:
                def store_k(refs, plane, part):
                    refs[1][plane] = part.astype(BF16)
                to_planes(2, cn, store_k)
    vals = y[:, MIX_WIDTH + KV_WIDTH:]
    v_ref[...] = vals.astype(BF16)

    def store_v(refs, plane, part):
        refs[2][plane] = part.astype(BF16)
    to_planes(3, vals[:, MIXER_C * LANES:(MIXER_C + 1) * LANES], store_v)


def _qkv_proj(x2, nw, w, gain, cos, sin, batch, seq, tm):
    n = x2.shape[0]
    tiles_per_seq = seq // tm
    out_specs = [
        pl.BlockSpec((tm, N_MIXERS * Q_SLOTS), lambda i: (i, 0)),
        pl.BlockSpec((tm, KV_WIDTH), lambda i: (i, 0)),
        pl.BlockSpec((tm, KV_WIDTH), lambda i: (i, 0)),
    ]
    out_shape = [
        jax.ShapeDtypeStruct((n, N_MIXERS * Q_SLOTS), BF16),
        jax.ShapeDtypeStruct((n, KV_WIDTH), BF16),
        jax.ShapeDtypeStruct((n, KV_WIDTH), BF16),
    ]
    for dil in PLANE_DILS:
        for width in (Q_SLOTS, MIXER_KV, MIXER_KV):
            out_specs.append(pl.BlockSpec((None, dil, tm // dil, width),
                                          lambda i: (i // tiles_per_seq, 0, i % tiles_per_seq, 0)))
            out_shape.append(jax.ShapeDtypeStruct((batch, dil, seq // dil, width), BF16))
    return pl.pallas_call(
        functools.partial(_qkv_kernel, tm=tm),
        grid=(n // tm,),
        in_specs=[
            pl.BlockSpec((tm, D_MODEL), lambda i: (i, 0)),
            _resident((1, D_MODEL)),
            _resident((D_MODEL, IN_WIDTH)),
            _resident((1, MIX_WIDTH + KV_WIDTH)),
            pl.BlockSpec((tm, LANES), lambda i: (i % tiles_per_seq, 0)),
            pl.BlockSpec((tm, LANES), lambda i: (i % tiles_per_seq, 0)),
        ],
        out_specs=out_specs,
        out_shape=out_shape,
        scratch_shapes=[pltpu.VMEM((4, tm, LANES), F32)],
        compiler_params=_cparams(("parallel",)),
        name="qkv_proj",
    )(x2, nw, w, gain, cos, sin)


def _stack_heads(qs):
    return jnp.concatenate([qs[:, h * LANES:(h + 1) * LANES] for h in range(Q_HEADS)], axis=0)


def _unstack_heads(o, rows, lo):
    c0 = jnp.where(lo, o[0:rows], o[2 * rows:3 * rows])
    c1 = jnp.where(lo, o[rows:2 * rows], o[3 * rows:4 * rows])
    return c0, c1


LOCAL_TOKENS_PER_STEP = 2048


def _local_kernel(*refs, dil, qb, kw, tq, length, radius, nbr, has_sink, want_lse, unroll, mxu_rowsum):
    it = iter(refs)
    q_ref, k_ref, v_ref, tab_ref = next(it), next(it), next(it), next(it)
    sink_ref = next(it) if has_sink else None
    o_ref = next(it)
    lse_ref = next(it) if want_lse else None
    i = pl.program_id(1)
    nsub = tq // qb
    nblk = length // qb
    lo = lax.broadcasted_iota(jnp.int32, (1, LANES), 1) < HEAD_DIM

    def sub_block(f):
        plane, sb = (0, f) if dil == 1 else (f // nsub, f % nsub)
        blk = i * nsub + sb
        if nbr:
            rows = length // GRID_W
            r0 = jnp.clip(blk - NA_ROWS // 2, 0, rows - NA_ROWS)
            kstart = r0 * GRID_W
            var = blk - r0
        else:
            kstart = jnp.clip(blk * qb - radius, 0, length - kw)
            var = jnp.where(blk == 0, 0, jnp.where(blk == nblk - 1, 2, 1))
        kstart = pl.multiple_of(kstart, qb)
        qoff = pl.multiple_of(sb * qb, qb)
        qst = _stack_heads(q_ref[plane, pl.ds(qoff, qb), :])
        kb = k_ref[plane, pl.ds(kstart, kw), :]
        vb = v_ref[plane, pl.ds(kstart, kw), :]
        s = lax.dot_general(qst, kb, (((1,), (1,)), ((), ())), preferred_element_type=F32)
        s = s + tab_ref[var]
        n = Q_HEADS * qb
        m = jnp.broadcast_to(jnp.max(s, axis=-1, keepdims=True), (n, LANES))
        if has_sink:
            sk = sink_ref[...]
            m = jnp.maximum(m, sk)
        chunks = [jnp.exp2(s[:, c * LANES:(c + 1) * LANES] - m) for c in range(kw // LANES)]
        if kw % LANES:
            chunks.append(jnp.exp2(s[:, kw - kw % LANES:] - m[:, :kw % LANES]))
        p32 = jnp.concatenate(chunks, axis=-1)
        p = p32.astype(BF16)
        if mxu_rowsum:
            both = jnp.dot(p, jnp.concatenate([vb, jnp.ones((kw, LANES), BF16)], axis=-1),
                           preferred_element_type=F32)
            acc, den = both[:, :LANES], both[:, LANES:]
        else:
            den = jnp.broadcast_to(jnp.sum(p32, axis=-1, keepdims=True), (n, LANES))
            acc = jnp.dot(p, vb, preferred_element_type=F32)
        if has_sink:
            den = den + jnp.exp2(sk - m)
        out_rows = pl.ds(qoff, qb) if dil == 1 else pl.ds(qoff * dil + plane, qb, stride=dil)
        c0, c1 = _unstack_heads(acc / den, qb, lo)
        o_ref[0, out_rows, :] = c0
        o_ref[1, out_rows, :] = c1
        if want_lse:
            l0, l1 = _unstack_heads(m + jnp.log2(den), qb, lo)
            lse_ref[0, out_rows, :] = l0
            lse_ref[1, out_rows, :] = l1

    def group(gi, carry):
        for u in range(unroll):
            sub_block(gi * unroll + u)
        return carry

    lax.fori_loop(0, dil * nsub // unroll, group, 0)


def _local_attention(q, k, v, tab, sink_tab, *, batch, seq, dil, col, qb, kw, radius, name, unroll,
                     mxu_rowsum, nbr=False, want_lse=False):
    length = seq // dil
    tq = min(length, LOCAL_TOKENS_PER_STEP // dil)
    has_sink = sink_tab is not None
    kv_spec = pl.BlockSpec((None, dil, length, MIXER_KV), lambda b, i: (b, 0, 0, col))
    in_specs = [pl.BlockSpec((None, dil, tq, Q_SLOTS), lambda b, i: (b, 0, i, col)), kv_spec, kv_spec,
                _resident(tab.shape)]
    args = [q, k, v, tab]
    if has_sink:
        in_specs.append(_resident(sink_tab.shape))
        args.append(sink_tab)
    o_spec = pl.BlockSpec((None, MIXER_Q // LANES, tq * dil, LANES), lambda b, i: (b, 0, i, 0))
    o_shape = jax.ShapeDtypeStruct((batch, MIXER_Q // LANES, seq, LANES), F32)
    kern = functools.partial(_local_kernel, dil=dil, qb=qb, kw=kw, tq=tq, length=length, radius=radius,
                             nbr=nbr, has_sink=has_sink, want_lse=want_lse, mxu_rowsum=mxu_rowsum,
                             unroll=min(unroll, dil * (tq // qb)))
    outs = pl.pallas_call(
        kern,
        grid=(batch, length // tq),
        in_specs=in_specs,
        out_specs=[o_spec, o_spec] if want_lse else [o_spec],
        out_shape=[o_shape, o_shape] if want_lse else [o_shape],
        compiler_params=_cparams(("parallel", "arbitrary")),
        name=name,
    )(*args)
    return list(outs)


DENSE_CHAINS = 4


def _dense_kernel(q_ref, k_ref, v_ref, o_ref, *, tq, tk, seq):
    hq = tq // DENSE_CHAINS
    rows = Q_HEADS * hq
    qsts = [_stack_heads(q_ref[c * hq:(c + 1) * hq, :]) for c in range(DENSE_CHAINS)]

    def body(j, carry):
        koff = pl.multiple_of(j * tk, tk)
        kb = k_ref[pl.ds(koff, tk), :]
        vb = v_ref[pl.ds(koff, tk), :]
        out = []
        for c in range(DENSE_CHAINS):
            m, l, acc = carry[c]
            s = lax.dot_general(qsts[c], kb, (((1,), (1,)), ((), ())), preferred_element_type=F32)
            m_new = jnp.maximum(m, jnp.max(s, axis=-1, keepdims=True))
            alpha = jnp.exp2(m - m_new)
            p = jnp.exp2(s - m_new)
            l = alpha * l + jnp.sum(p, axis=-1, keepdims=True)
            acc = alpha * acc + jnp.dot(p.astype(BF16), vb, preferred_element_type=F32)
            out.append((m_new, l, acc))
        return tuple(out)

    init = tuple((jnp.full((rows, 1), NEG_INF, F32), jnp.zeros((rows, 1), F32), jnp.zeros((rows, LANES), F32))
                 for _ in range(DENSE_CHAINS))
    final = lax.fori_loop(0, seq // tk, body, init, unroll=2)
    lo = lax.broadcasted_iota(jnp.int32, (1, LANES), 1) < HEAD_DIM
    for c, (_, l, acc) in enumerate(final):
        c0, c1 = _unstack_heads(acc / l, hq, lo)
        o_ref[c * hq:(c + 1) * hq, 0:LANES] = c0
        o_ref[c * hq:(c + 1) * hq, LANES:2 * LANES] = c1


def _dense_attention(q, k, v, *, batch, seq, mixer, tq, tk):
    q3 = q.reshape(batch, seq, N_MIXERS * Q_SLOTS)
    k3 = k.reshape(batch, seq, KV_WIDTH)
    v3 = v.reshape(batch, seq, KV_WIDTH)
    out = pl.pallas_call(
        functools.partial(_dense_kernel, tq=tq, tk=tk, seq=seq),
        grid=(batch, seq // tq),
        in_specs=[
            pl.BlockSpec((None, tq, Q_SLOTS), lambda b, i: (b, i, mixer)),
            pl.BlockSpec((None, seq, MIXER_KV), lambda b, i: (b, 0, mixer)),
            pl.BlockSpec((None, seq, MIXER_KV), lambda b, i: (b, 0, mixer)),
        ],
        out_specs=pl.BlockSpec((None, tq, MIXER_Q), lambda b, i: (b, i, 0)),
        out_shape=jax.ShapeDtypeStruct((batch, seq, MIXER_Q), F32),
        compiler_params=_cparams(("parallel", "arbitrary")),
        name="dense_attn",
    )(q3, k3, v3)
    return out.reshape(batch * seq, MIXER_Q)


def _outproj_kernel(x_ref, oa_ref, ob_ref, oc1_ref, lc1_ref, oc2_ref, lc2_ref, oc3_ref, lc3_ref,
                    od_ref, gain_ref, w_ref, y_ref):
    def halves(ref):
        return jnp.concatenate([ref[0], ref[1]], axis=-1)

    l1, l2, l3 = halves(lc1_ref), halves(lc2_ref), halves(lc3_ref)
    mx = jnp.maximum(jnp.maximum(l1, l2), l3)
    e1, e2, e3 = jnp.exp2(l1 - mx), jnp.exp2(l2 - mx), jnp.exp2(l3 - mx)
    oc = (e1 * halves(oc1_ref) + e2 * halves(oc2_ref) + e3 * halves(oc3_ref)) / (e1 + e2 + e3)
    normed = []
    for mixer, o in enumerate((halves(oa_ref), halves(ob_ref), oc, od_ref[...])):
        ms = jnp.mean(o * o, axis=-1, keepdims=True)
        gain = gain_ref[:, mixer * MIXER_Q:(mixer + 1) * MIXER_Q]
        normed.append((o * lax.rsqrt(ms + EPS) * gain).astype(BF16))
    y_ref[...] = x_ref[...] + jnp.dot(jnp.concatenate(normed, axis=-1), w_ref[...],
                                      preferred_element_type=F32)


def _out_proj(x2, local_parts, o_dense, gain, w, seq, tm):
    n = x2.shape[0]
    tiles_per_seq = seq // tm
    row_spec = pl.BlockSpec((tm, D_MODEL), lambda i: (i, 0))
    local_spec = pl.BlockSpec((None, MIXER_Q // LANES, tm, LANES),
                              lambda i: (i // tiles_per_seq, 0, i % tiles_per_seq, 0))
    dense_spec = pl.BlockSpec((tm, MIXER_Q), lambda i: (i, 0))
    return pl.pallas_call(
        _outproj_kernel,
        grid=(n // tm,),
        in_specs=[row_spec] + [local_spec] * len(local_parts) + [dense_spec, _resident((1, MIX_WIDTH)),
                                                                 _resident((MIX_WIDTH, D_MODEL))],
        out_specs=row_spec,
        out_shape=jax.ShapeDtypeStruct((n, D_MODEL), F32),
        compiler_params=_cparams(("parallel",)),
        name="out_proj",
    )(x2, *local_parts, o_dense, gain, w)


def _ffn_kernel(x_ref, xp_ref, xn_ref, nw_ref, wg_ref, wv_ref, cw_ref, cb_ref, wd_ref, y_ref,
                h_ref, act_ref, *, tm, tiles_per_seq):
    i = pl.program_id(0)
    first = (i % tiles_per_seq) == 0
    last = (i % tiles_per_seq) == tiles_per_seq - 1
    x = x_ref[...]
    xp = jnp.where(first, 0.0, xp_ref[...])
    xn = jnp.where(last, 0.0, xn_ref[...])
    xe = jnp.concatenate([xp, x, xn], axis=0)
    ms = jnp.mean(xe * xe, axis=-1, keepdims=True)
    h_ref[...] = (xe * lax.rsqrt(ms + EPS) * nw_ref[...]).astype(BF16)
    ext = tm + 2 * HALO

    for f in range(N_FF_CHUNKS):
        h = h_ref[...]
        g = jnp.dot(h, wg_ref[f], preferred_element_type=F32)
        u = jnp.dot(h, wv_ref[f], preferred_element_type=F32)
        cw = cw_ref[f]
        g_prev = pltpu.roll(g, 1, 0)
        g_next = pltpu.roll(g, ext - 1, 0)
        gc = cw[0:1] * g_prev + cw[1:2] * g + cw[2:3] * g_next + cb_ref[f]
        gc = gc[HALO:HALO + tm]
        act = 0.5 * gc * (1.0 + lax.erf(gc * (2.0 ** -0.5)))
        act_ref[:, f * FF_CHUNK:(f + 1) * FF_CHUNK] = (act * u[HALO:HALO + tm]).astype(BF16)
    y_ref[...] = x + jnp.dot(act_ref[...], wd_ref[...], preferred_element_type=F32)


def _ffn(x2, nw, wg, wv, cw, cb, wd, seq, tm):
    n = x2.shape[0]
    tiles_per_seq = seq // tm
    blocks_per_tile = tm // HALO
    n_halo_blocks = n // HALO
    return pl.pallas_call(
        functools.partial(_ffn_kernel, tm=tm, tiles_per_seq=tiles_per_seq),
        grid=(n // tm,),
        in_specs=[
            pl.BlockSpec((tm, D_MODEL), lambda i: (i, 0)),
            pl.BlockSpec((HALO, D_MODEL), lambda i: (jnp.maximum(i * blocks_per_tile - 1, 0), 0)),
            pl.BlockSpec((HALO, D_MODEL),
                         lambda i: (jnp.minimum((i + 1) * blocks_per_tile, n_halo_blocks - 1), 0)),
            _resident((1, D_MODEL)),
            _resident((N_FF_CHUNKS, D_MODEL, FF_CHUNK)),
            _resident((N_FF_CHUNKS, D_MODEL, FF_CHUNK)),
            _resident((N_FF_CHUNKS, 3, FF_CHUNK)),
            _resident((N_FF_CHUNKS, 1, FF_CHUNK)),
            _resident((D_FF, D_MODEL)),
        ],
        out_specs=pl.BlockSpec((tm, D_MODEL), lambda i: (i, 0)),
        out_shape=jax.ShapeDtypeStruct((n, D_MODEL), F32),
        scratch_shapes=[pltpu.VMEM((tm + 2 * HALO, D_MODEL), BF16), pltpu.VMEM((tm, D_FF), BF16)],
        compiler_params=_cparams(("parallel",)),
        name="conv_glu_ffn",
    )(x2, x2, x2, nw, wg, wv, cw, cb, wd)


def _alibi_slopes():
    n = 2 * Q_HEADS
    s = 2.0 ** (-8.0 * np.arange(1, n + 1, dtype=np.float64) / n)
    return s[0::2], s[1::2]


def _band_table(radius, qb, kw, slopes, step):
    qi = np.arange(qb)[:, None]
    ki = np.arange(kw)[None, :]
    out = np.empty((3, Q_HEADS, qb, kw), np.float32)
    for var, off in enumerate((0, radius, 2 * radius)):
        dist = np.abs(qi + off - ki)
        for h in range(Q_HEADS):
            out[var, h] = np.where(dist <= radius, -slopes[h] * step * dist * LOG2E, NEG_INF)
    return jnp.asarray(out.reshape(3, Q_HEADS * qb, kw))


def _nbr_table(rpb):
    by_row = jnp.stack([rpb[:, NA_ROWS - 1 - var:2 * NA_ROWS - 1 - var, :] for var in range(NA_ROWS)], axis=1)
    qc = np.arange(GRID_W)[:, None]
    c = np.arange(GRID_W)[None, :]
    cstart = np.clip(qc - NA_COLS // 2, 0, GRID_W - NA_COLS)
    valid = (c >= cstart) & (c < cstart + NA_COLS)
    dc = c - qc + (NA_COLS - 1)
    pick = ((dc[None] == np.arange(2 * NA_COLS - 1)[:, None, None]) & valid[None]).astype(np.float32)
    t = jnp.einsum("hvjd,dqc->vhqjc", by_row.astype(F32), jnp.asarray(pick), precision=lax.Precision.HIGHEST)
    t = jnp.where(valid[None, None, :, None, :], t * LOG2E, NEG_INF)
    return t.reshape(NA_ROWS, Q_HEADS * GRID_W, NA_ROWS * GRID_W)


def _rope_tables(seq):
    rot_row = HEAD_DIM // 2
    rot_col = HEAD_DIM - rot_row
    t = jnp.arange(seq)
    row = (t // GRID_W).astype(F32)
    col = (t % GRID_W).astype(F32)
    f_row = ROPE_THETA ** (-jnp.arange(0, rot_row, 2, dtype=F32) / rot_row)
    f_col = ROPE_THETA ** (-jnp.arange(0, rot_col, 2, dtype=F32) / rot_col)
    ang = jnp.concatenate([row[:, None] * f_row[None, :], col[:, None] * f_col[None, :]], axis=-1)
    cos, sin = jnp.cos(ang), jnp.sin(ang)
    cos_t = jnp.tile(jnp.concatenate([cos, cos], axis=-1), (1, LANES // HEAD_DIM))
    sin_t = jnp.tile(jnp.concatenate([-sin, sin], axis=-1), (1, LANES // HEAD_DIM))
    return cos_t, sin_t


def _even_odd(a):
    return jnp.concatenate([a[..., 0::2], a[..., 1::2]], axis=-1)


def _rotary_last_mixer(a, axis):
    head, tail = lax.slice_in_dim(a, 0, N_MIXERS - 1, axis=axis), lax.slice_in_dim(a, N_MIXERS - 1, N_MIXERS, axis=axis)
    return jnp.concatenate([head, _even_odd(tail)], axis=axis)


def _layer_params(norm1_w, w_in, q_norm_w, k_norm_w, sink_a, rpb_b, out_norm_w, w_out,
                  norm2_w, w_gate, w_val, conv_w, conv_b, w_down):
    groups = Q_HEADS // KV_HEADS
    wq = w_in[:, :MIX_WIDTH].reshape(D_MODEL, N_MIXERS, KV_HEADS, groups, HEAD_DIM)
    wq = _rotary_last_mixer(jnp.swapaxes(wq, 2, 3), 1).reshape(D_MODEL, MIX_WIDTH)
    wk = w_in[:, MIX_WIDTH:MIX_WIDTH + KV_WIDTH].reshape(D_MODEL, N_MIXERS, KV_HEADS, HEAD_DIM)
    wk = _rotary_last_mixer(wk, 1).reshape(D_MODEL, KV_WIDTH)
    w_in_p = jnp.concatenate([wq, wk, w_in[:, MIX_WIDTH + KV_WIDTH:]], axis=1).astype(BF16)
    qg = _rotary_last_mixer(q_norm_w, 0) * LOG2E
    kg = _rotary_last_mixer(k_norm_w, 0) * HEAD_DIM ** 0.5
    gain = jnp.concatenate([jnp.broadcast_to(qg[:, None, :], (N_MIXERS, Q_HEADS, HEAD_DIM)).reshape(-1),
                            jnp.broadcast_to(kg[:, None, :], (N_MIXERS, KV_HEADS, HEAD_DIM)).reshape(-1)])
    out_gain = jnp.swapaxes(out_norm_w.reshape(N_MIXERS, KV_HEADS, groups, HEAD_DIM), 1, 2)
    w_out_p = jnp.swapaxes(w_out.reshape(N_MIXERS, KV_HEADS, groups, HEAD_DIM, D_MODEL), 1, 2)
    return dict(
        norm1_w=norm1_w.reshape(1, D_MODEL),
        w_in=w_in_p,
        qk_gain=gain.reshape(1, MIX_WIDTH + KV_WIDTH),
        sink=sink_a,
        rpb=rpb_b,
        out_gain=out_gain.reshape(1, MIX_WIDTH),
        w_out=w_out_p.reshape(MIX_WIDTH, D_MODEL).astype(BF16),
        norm2_w=norm2_w.reshape(1, D_MODEL),
        w_gate=jnp.transpose(w_gate.reshape(D_MODEL, N_FF_CHUNKS, FF_CHUNK), (1, 0, 2)).astype(BF16),
        w_val=jnp.transpose(w_val.reshape(D_MODEL, N_FF_CHUNKS, FF_CHUNK), (1, 0, 2)).astype(BF16),
        conv_w=jnp.transpose(conv_w.reshape(3, N_FF_CHUNKS, FF_CHUNK), (1, 0, 2)),
        conv_b=conv_b.reshape(N_FF_CHUNKS, 1, FF_CHUNK),
        w_down=w_down.astype(BF16),
    )


BAND_A = dict(qb=128, kw=3 * 128, radius=WIN_A, unroll=4, mxu_rowsum=False)
BAND_C = dict(qb=64, kw=3 * 64, radius=64, unroll=32, mxu_rowsum=True)
NBR_B = dict(qb=GRID_W, kw=NA_ROWS * GRID_W, radius=0, nbr=True, unroll=16, mxu_rowsum=True)
ROW_TILE = 512
DENSE_TQ = 512
DENSE_TK = 2048


def _layer(x2, p, tabs, batch, seq):
    q, k, v, *planes = _qkv_proj(x2, p["norm1_w"], p["w_in"], p["qk_gain"], tabs["cos"], tabs["sin"],
                                 batch, seq, ROW_TILE)
    q4 = q.reshape(batch, 1, seq, N_MIXERS * Q_SLOTS)
    k4 = k.reshape(batch, 1, seq, KV_WIDTH)
    v4 = v.reshape(batch, 1, seq, KV_WIDTH)
    sink_tab = jnp.broadcast_to(jnp.repeat(p["sink"] * LOG2E, BAND_A["qb"])[:, None],
                                (Q_HEADS * BAND_A["qb"], LANES))
    (o_a,) = _local_attention(q4, k4, v4, tabs["band_a"], sink_tab, batch=batch, seq=seq, dil=1, col=0,
                              name="attn_a", **BAND_A)
    (o_b,) = _local_attention(q4, k4, v4, _nbr_table(p["rpb"]), None, batch=batch, seq=seq, dil=1, col=1,
                              name="attn_b", **NBR_B)
    c_parts = _local_attention(q4, k4, v4, tabs["band_c"][0], None, batch=batch, seq=seq, dil=1, col=MIXER_C,
                               want_lse=True, name="attn_c1", **BAND_C)
    for n, dil in enumerate(PLANE_DILS):
        qd, kd, vd = planes[3 * n:3 * n + 3]
        c_parts += _local_attention(qd, kd, vd, tabs["band_c"][n + 1], None, batch=batch, seq=seq, dil=dil,
                                    col=0, want_lse=True, name=f"attn_c{dil}", **BAND_C)
    o_d = _dense_attention(q, k, v, batch=batch, seq=seq, mixer=MIXER_D, tq=DENSE_TQ, tk=min(seq, DENSE_TK))
    x2 = _out_proj(x2, [o_a, o_b] + c_parts, o_d, p["out_gain"], p["w_out"], seq, ROW_TILE)
    return _ffn(x2, p["norm2_w"], p["w_gate"], p["w_val"], p["conv_w"], p["conv_b"], p["w_down"], seq, ROW_TILE)


def kernel(x_prompt, x_sample, norm1_w, w_in, q_norm_w, k_norm_w, sink_a, rpb_b, out_norm_w, w_out,
           norm2_w, w_gate, w_val, conv_w, conv_b, w_down):
    depth = norm1_w.shape[0]
    slopes_a, slopes_c = _alibi_slopes()
    band_a = _band_table(BAND_A["radius"], BAND_A["qb"], BAND_A["kw"], slopes_a, 1)
    band_c = [_band_table(BAND_C["radius"], BAND_C["qb"], BAND_C["kw"], slopes_c, dil) for _, dil in DILATIONS]
    layers = [_layer_params(norm1_w[l], w_in[l], q_norm_w[l], k_norm_w[l], sink_a[l], rpb_b[l],
                            out_norm_w[l], w_out[l], norm2_w[l], w_gate[l], w_val[l], conv_w[l],
                            conv_b[l], w_down[l]) for l in range(depth)]
    outs = []
    for x in (x_prompt, x_sample):
        batch, seq, _ = x.shape
        cos, sin = _rope_tables(seq)
        tabs = dict(cos=cos, sin=sin, band_a=band_a, band_c=band_c)
        x2 = x.reshape(batch * seq, D_MODEL)
        for p in layers:
            x2 = _layer(x2, p, tabs, batch, seq)
        outs.append(x2.reshape(batch, seq, D_MODEL))
    return tuple(outs)
```
